```python
import math
import jax, jax.numpy as jnp
from jax import lax
import numpy as np

D_MODEL = 1024
BATCH = 4
SEQ = 8192
DEPTH = 1

CTX_LEN = 256
GRID_W = 64
EPS = 1e-6
MIX_WIDTH = D_MODEL
GLA_WIDTH = MIX_WIDTH // 2
DIFF_WIDTH = MIX_WIDTH - GLA_WIDTH
GLA_HEADS = 4
GLA_DV = GLA_WIDTH // GLA_HEADS
GLA_DK = GLA_DV // 2
GLA_QK = GLA_HEADS * GLA_DK
GLA_GATE_RANK = 16
GLA_GATE_NORM = 16.0
GLA_CHUNK = 64
DIFF_HEADS = 4
DIFF_DV = DIFF_WIDTH // DIFF_HEADS
DIFF_DH = DIFF_DV // 2
ROPE_BASE = 10000.0
ROPE_AXIS_DIM = DIFF_DH // 2
Q_BLOCK = 128
FFN_HIDDEN = ((8 * D_MODEL // 3 + 255) // 256) * 256
IN_SIZES = (GLA_QK, GLA_QK, GLA_WIDTH, GLA_WIDTH, 2 * GLA_GATE_RANK, DIFF_WIDTH, DIFF_WIDTH, DIFF_WIDTH)
W_IN_COLS = sum(IN_SIZES)

kernel_name = "hymba_gla_diffattn_dit_block"


def rmsnorm(x, g):
    xf = x.astype(jnp.float32)
    y = xf * lax.rsqrt(jnp.mean(xf * xf, axis=-1, keepdims=True) + EPS)
    return (y * g.astype(jnp.float32)).astype(x.dtype)


def modulate(x, g, shift, scale):
    return rmsnorm(x, g) * (1 + scale) + shift


def split_heads(t, n_heads):
    b, n, _ = t.shape
    return t.reshape(b, n, n_heads, -1).transpose(0, 2, 1, 3)


def merge_heads(t):
    b, h, n, d = t.shape
    return t.transpose(0, 2, 1, 3).reshape(b, n, h * d)


def adaln_params(cond, w_mod, b_mod):
    return jnp.split(jax.nn.silu(cond) @ w_mod + b_mod, 6, axis=-1)


def axial_rope_tables(n_tokens):
    rows = n_tokens // GRID_W
    t = jnp.arange(rows * GRID_W)
    row = (t // GRID_W).astype(jnp.float32)
    col = (t % GRID_W).astype(jnp.float32)
    n_freq = ROPE_AXIS_DIM // 2
    inv_freq = ROPE_BASE ** (-jnp.arange(n_freq, dtype=jnp.float32) / n_freq)
    ang = jnp.stack([row[:, None] * inv_freq, col[:, None] * inv_freq], axis=1)
    return jnp.cos(ang), jnp.sin(ang)


def apply_axial_rope(x, cos, sin):
    xr = x.reshape(x.shape[:-1] + (2, 2, ROPE_AXIS_DIM // 2))
    x1, x2 = xr[..., 0, :], xr[..., 1, :]
    cs = cos[:, None].astype(x.dtype)
    sn = sin[:, None].astype(x.dtype)
    out = jnp.stack([x1 * cs - x2 * sn, x2 * cs + x1 * sn], axis=-2)
    return out.reshape(x.shape)


def project(h, w_in):
    points = np.cumsum(IN_SIZES)[:-1].tolist()
    return jnp.split(h @ w_in, points, axis=-1)


def gla_inputs(parts, gate_up, gate_bias):
    gq, gk, gv, gr, gdown = parts[:5]
    q = split_heads(gq, GLA_HEADS) * (GLA_DK ** -0.5)
    k = split_heads(gk, GLA_HEADS)
    v = split_heads(gv, GLA_HEADS)
    down = gdown.astype(jnp.float32).reshape(gdown.shape[:-1] + (2, GLA_GATE_RANK))
    logits = jnp.einsum('bnzr,zrk->bnzk', down, gate_up.astype(jnp.float32)) + gate_bias.astype(jnp.float32)
    log_alpha = jax.nn.log_sigmoid(logits) / GLA_GATE_NORM
    g_f = split_heads(log_alpha[:, :, 0], GLA_HEADS)
    g_b = split_heads(log_alpha[:, :, 1], GLA_HEADS)
    return q, k, v, gr, g_f, g_b


def gla_chunk_scan(q, k, v, g, s0):
    b, h, n, dk = q.shape
    dv = v.shape[-1]
    nc = n // GLA_CHUNK

    def chunks(t):
        return jnp.moveaxis(t.astype(jnp.float32).reshape(b, h, nc, GLA_CHUNK, t.shape[-1]), 2, 0)

    mask = jnp.tril(jnp.ones((GLA_CHUNK, GLA_CHUNK), dtype=bool))

    def step(state, inp):
        qc, kc, vc, gc = inp
        cum = jnp.cumsum(gc, axis=2)
        o_inter = jnp.einsum('bhcd,bhde->bhce', qc * jnp.exp(cum), state)
        rel = cum[:, :, :, None, :] - cum[:, :, None, :, :]
        decay = jnp.exp(jnp.where(mask[:, :, None], rel, -jnp.inf))
        attn = jnp.einsum('bhid,bhjd,bhijd->bhij', qc, kc, decay)
        o = o_inter + jnp.einsum('bhij,bhje->bhie', attn, vc)
        last = cum[:, :, -1:, :]
        state = jnp.exp(last[:, :, 0, :, None]) * state + jnp.einsum('bhcd,bhce->bhde', kc * jnp.exp(last - cum), vc)
        return state, o

    s_final, o = lax.scan(step, s0, (chunks(q), chunks(k), chunks(v), chunks(g)))
    o = jnp.moveaxis(o, 0, 2).reshape(b, h, n, dv)
    return s_final, o.astype(v.dtype)


def gla_bidir(q, k, v, g_f, g_b, s0_f, s0_b):
    flip = lambda t: jnp.flip(t, axis=2)
    s_f, o_f = gla_chunk_scan(q, k, v, g_f, s0_f)
    s_b, o_b = gla_chunk_scan(flip(q), flip(k), flip(v), flip(g_b), s0_b)
    return s_f, s_b, o_f + flip(o_b)


def gla_merge(o, r, norm_g):
    return merge_heads(rmsnorm(o, norm_g)) * jax.nn.silu(r)


def diff_qkv(parts, q_norm_g, k_norm_g):
    dq, dk, dv = parts[5:]
    b, n, _ = dq.shape
    q = rmsnorm(dq.reshape(b, n, DIFF_HEADS, 2, DIFF_DH).transpose(0, 2, 1, 3, 4), q_norm_g)
    k = rmsnorm(dk.reshape(b, n, DIFF_HEADS, 2, DIFF_DH).transpose(0, 2, 1, 3, 4), k_norm_g)
    v = split_heads(dv, DIFF_HEADS)
    return q, k, v


def diff_attend(q, k_all, v_all, lam):
    b, h, n = q.shape[:3]
    nb = n // Q_BLOCK
    q_blocks = jnp.moveaxis(q.reshape(b, h, nb, Q_BLOCK, 2, DIFF_DH), 2, 0)

    def one_block(qb):
        s = jnp.einsum('bhqcd,bhkcd->bhcqk', qb, k_all).astype(jnp.float32) * (DIFF_DH ** -0.5)
        p = jax.nn.softmax(s, axis=-1)
        a = (p[:, :, 0] - lam.astype(jnp.float32) * p[:, :, 1]).astype(v_all.dtype)
        return jnp.einsum('bhqk,bhkd->bhqd', a, v_all)

    o = lax.map(one_block, q_blocks)
    return jnp.moveaxis(o, 0, 2).reshape(b, h, n, DIFF_DV)


def diff_merge(o, norm_g, lam_init):
    return merge_heads(rmsnorm(o, norm_g) * (1.0 - lam_init))


def swiglu(h, w_in, w_out):
    gate, up = jnp.split(h @ w_in, 2, axis=-1)
    return (jax.nn.silu(gate) * up) @ w_out


def setup_inputs(seed: int = 0) -> dict:
    key = jax.random.key(seed)
    ks = jax.random.split(key, 20)
    f32 = jnp.float32
    nrm = lambda k, shape, s: jax.random.normal(k, shape, f32) * s
    return {
        "x": nrm(ks[0], (BATCH, SEQ, D_MODEL), 1.0),
        "c": nrm(ks[1], (BATCH, D_MODEL), 1.0),
        "ctx": nrm(ks[2], (BATCH, CTX_LEN, D_MODEL), 1.0),
        "c_ctx": nrm(ks[3], (D_MODEL,), 1.0),
        "w_mod": nrm(ks[4], (DEPTH, D_MODEL, 6 * D_MODEL), 0.5 * D_MODEL ** -0.5),
        "b_mod": nrm(ks[5], (DEPTH, 6 * D_MODEL), 0.01),
        "norm1_g": 1.0 + nrm(ks[6], (DEPTH, D_MODEL), 0.02),
        "w_in": nrm(ks[7], (DEPTH, D_MODEL, W_IN_COLS), D_MODEL ** -0.5),
        "gla_gate_up": nrm(ks[8], (DEPTH, 2, GLA_GATE_RANK, GLA_QK), GLA_GATE_RANK ** -0.5),
        "gla_gate_bias": nrm(ks[9], (DEPTH, 2, GLA_QK), 0.1),
        "gla_norm_g": 1.0 + nrm(ks[10], (DEPTH, GLA_DV), 0.02),
        "diff_q_norm_g": 1.0 + nrm(ks[11], (DEPTH, DIFF_DH), 0.02),
        "diff_k_norm_g": 1.0 + nrm(ks[12], (DEPTH, DIFF_DH), 0.02),
        "diff_lambda_q": nrm(ks[13], (DEPTH, 2, DIFF_DH), 0.1),
        "diff_lambda_k": nrm(ks[14], (DEPTH, 2, DIFF_DH), 0.1),
        "diff_norm_g": 1.0 + nrm(ks[15], (DEPTH, DIFF_DV), 0.02),
        "w_out": nrm(ks[16], (DEPTH, MIX_WIDTH, D_MODEL), MIX_WIDTH ** -0.5),
        "norm2_g": 1.0 + nrm(ks[17], (DEPTH, D_MODEL), 0.02),
        "w_ffn_in": nrm(ks[18], (DEPTH, D_MODEL, 2 * FFN_HIDDEN), D_MODEL ** -0.5),
        "w_ffn_out": nrm(ks[19], (DEPTH, FFN_HIDDEN, D_MODEL), FFN_HIDDEN ** -0.5),
    }


def reference(x, c, ctx, c_ctx, w_mod, b_mod, norm1_g, w_in, gla_gate_up, gla_gate_bias, gla_norm_g,
              diff_q_norm_g, diff_k_norm_g, diff_lambda_q, diff_lambda_k, diff_norm_g, w_out, norm2_g,
              w_ffn_in, w_ffn_out):
    b = x.shape[0]
    cos, sin = axial_rope_tables(x.shape[1])
    zero_state = jnp.zeros((b, GLA_HEADS, GLA_DK, GLA_DV), jnp.float32)
    for l in range(DEPTH):
        lam_init = 0.8 - 0.6 * math.exp(-0.3 * l)
        lam = (jnp.exp(jnp.sum(diff_lambda_q[l, 0] * diff_lambda_k[l, 0]))
               - jnp.exp(jnp.sum(diff_lambda_q[l, 1] * diff_lambda_k[l, 1])) + lam_init)
        sh1, sc1, gt1, sh2, sc2, gt2 = [m[:, None, :] for m in adaln_params(c, w_mod[l], b_mod[l])]
        csh1, csc1, cgt1, csh2, csc2, cgt2 = adaln_params(c_ctx, w_mod[l], b_mod[l])

        pc = project(modulate(ctx, norm1_g[l], csh1, csc1), w_in[l])
        qg_c, kg_c, vg_c, r_c, gf_c, gb_c = gla_inputs(pc, gla_gate_up[l], gla_gate_bias[l])
        s_f, s_b, og_c = gla_bidir(qg_c, kg_c, vg_c, gf_c, gb_c, zero_state, zero_state)
        qd_c, kd_c, vd_c = diff_qkv(pc, diff_q_norm_g[l], diff_k_norm_g[l])

        px = project(modulate(x, norm1_g[l], sh1, sc1), w_in[l])
        qg, kg, vg, r, gf, gb = gla_inputs(px, gla_gate_up[l], gla_gate_bias[l])
        _, _, og = gla_bidir(qg, kg, vg, gf, gb, s_f, s_b)
        qd, kd, vd = diff_qkv(px, diff_q_norm_g[l], diff_k_norm_g[l])
        qd = apply_axial_rope(qd, cos, sin)
        kd = apply_axial_rope(kd, cos, sin)
        od = diff_attend(qd, jnp.concatenate([kd_c, kd], axis=2), jnp.concatenate([vd_c, vd], axis=2), lam)
        mix = jnp.concatenate([gla_merge(og, r, gla_norm_g[l]), diff_merge(od, diff_norm_g[l], lam_init)], axis=-1) @ w_out[l]
        x = x + gt1 * mix
        x = x + gt2 * swiglu(modulate(x, norm2_g[l], sh2, sc2), w_ffn_in[l], w_ffn_out[l])

        if l < DEPTH - 1:
            od_c = diff_attend(qd_c, kd_c, vd_c, lam)
            mix_c = jnp.concatenate([gla_merge(og_c, r_c, gla_norm_g[l]), diff_merge(od_c, diff_norm_g[l], lam_init)], axis=-1) @ w_out[l]
            ctx = ctx + cgt1 * mix_c
            ctx = ctx + cgt2 * swiglu(modulate(ctx, norm2_g[l], csh2, csc2), w_ffn_in[l], w_ffn_out[l])
    return x
```

```python
import functools
import math

import jax
import jax.numpy as jnp
from jax import lax
from jax.experimental import pallas as pl
from jax.experimental.pallas import tpu as pltpu

F32 = jnp.float32
BF16 = jnp.bfloat16
HIGHEST = lax.Precision.HIGHEST

EPS = 1e-6
GRID_W = 64
GLA_HEADS = 4
GLA_DK = 64
GLA_DV = 128
GLA_QK = GLA_HEADS * GLA_DK
GLA_WIDTH = GLA_HEADS * GLA_DV
GLA_GATE_RANK = 16
GLA_GATE_NORM = 16.0
GLA_CHUNK = 64
DIFF_HEADS = 4
DIFF_DH = 64
DIFF_DV = 128
DIFF_WIDTH = DIFF_HEADS * DIFF_DV
ROPE_BASE = 10000.0
ROPE_AXIS_DIM = DIFF_DH // 2
ROPE_HALF = ROPE_AXIS_DIM // 2
LANES = 128
DOWN_PAD = LANES

VMEM_LIMIT = 56 * 1024 * 1024


def _dot(a, b, precision=None):
    return jnp.dot(a, b, preferred_element_type=F32, precision=precision)


def _dot_nt(a, b):
    return lax.dot_general(a, b, (((1,), (1,)), ((), ())), preferred_element_type=F32)


def _dot_tn(a, b):
    return lax.dot_general(a, b, (((0,), (0,)), ((), ())), preferred_element_type=F32)


def _silu(x):
    return x / (1.0 + jnp.exp(-x))


def _mod_kernel(c_ref, w_ref, b_ref, o_ref):
    o_ref[...] = _dot(_silu(c_ref[...]), w_ref[...], HIGHEST) + b_ref[...]


def _mod_call(cond, w_mod, b_mod):
    rows, d = cond.shape
    cols = w_mod.shape[1]
    bn = cols // 4
    return pl.pallas_call(
        _mod_kernel,
        grid=(cols // bn,),
        in_specs=[pl.BlockSpec((rows, d), lambda j: (0, 0)),
                  pl.BlockSpec((d, bn), lambda j: (0, j)),
                  pl.BlockSpec((1, bn), lambda j: (0, j))],
        out_specs=pl.BlockSpec((rows, bn), lambda j: (0, j)),
        out_shape=jax.ShapeDtypeStruct((rows, cols), F32),
        compiler_params=pltpu.CompilerParams(dimension_semantics=("arbitrary",),
                                             vmem_limit_bytes=VMEM_LIMIT),
        name="adaln_mod",
    )(cond, w_mod, b_mod)


_C_GQ, _C_GK, _C_GV, _C_GR, _C_DQ, _C_DK, _C_DV, _C_DOWN, _C_END = (
    0, 256, 512, 1024, 1536, 2048, 2560, 3072, 3072 + DOWN_PAD)


def _swap_halves(y):
    lane = lax.broadcasted_iota(jnp.int32, y.shape, 1)
    upper = (lane & ROPE_HALF) != 0
    return jnp.where(upper, pltpu.roll(y, ROPE_HALF, 1), pltpu.roll(y, LANES - ROPE_HALF, 1))


def _proj_kernel(x_ref, sh_ref, sc_ref, g1_ref, w_ref, qg_ref, kg_ref, cos_ref, sin_ref, bd_ref,
                 gqk_o, gv_o, gr_o, down_o, dq_o, dk_o, dv_o, *, rope):
    x = x_ref[0]
    ms = jnp.mean(x * x, axis=-1, keepdims=True)
    h = x * lax.rsqrt(ms + EPS) * g1_ref[...]
    h = h * (1.0 + sc_ref[0]) + sh_ref[0]
    hb = h.astype(BF16)

    def mm(lo, hi):
        return _dot(hb, w_ref[:, lo:hi])

    gqk_o[0, :, 0:GLA_QK] = (mm(_C_GQ, _C_GK) * (GLA_DK ** -0.5)).astype(BF16)
    gqk_o[0, :, GLA_QK:2 * GLA_QK] = mm(_C_GK, _C_GV).astype(BF16)
    gv_o[0] = mm(_C_GV, _C_GR).astype(BF16)
    gr_o[0] = mm(_C_GR, _C_DQ).astype(BF16)
    dv_o[0] = mm(_C_DV, _C_DOWN).astype(BF16)
    down_o[0] = mm(_C_DOWN, _C_END)

    def qk_norm(y, g_ref, o_ref, scale):
        ss = _dot((y * y).astype(BF16), bd_ref[...])
        yn = y * lax.rsqrt(ss * (1.0 / DIFF_DH) + EPS) * g_ref[...]
        for s in range(DIFF_WIDTH // LANES):
            slab = yn[:, s * LANES:(s + 1) * LANES]
            if rope:
                slab = slab * cos_ref[...] + _swap_halves(slab) * sin_ref[...]
            o_ref[0, :, s * LANES:(s + 1) * LANES] = (slab * scale).astype(BF16)

    qk_norm(mm(_C_DQ, _C_DK), qg_ref, dq_o, DIFF_DH ** -0.5)
    qk_norm(mm(_C_DK, _C_DV), kg_ref, dk_o, 1.0)


def _proj_call(x, sh, sc, g1, w, qg, kg, cos_t, sin_t, bd, *, rope, tm):
    b, n, d = x.shape
    assert n % tm == 0
    tok = lambda width: pl.BlockSpec((1, tm, width), lambda i, t: (i, t, 0))
    const = lambda shape: pl.BlockSpec(shape, lambda i, t: tuple(0 for _ in shape))
    row = pl.BlockSpec((1, 1, d), lambda i, t: (i, 0, 0))
    outs = [(2 * GLA_QK, BF16), (GLA_WIDTH, BF16), (GLA_WIDTH, BF16), (DOWN_PAD, F32),
            (DIFF_WIDTH, BF16), (DIFF_WIDTH, BF16), (DIFF_WIDTH, BF16)]
    return pl.pallas_call(
        functools.partial(_proj_kernel, rope=rope),
        grid=(b, n // tm),
        in_specs=[tok(d), row, row, const((1, d)), const(w.shape), const((1, DIFF_WIDTH)),
                  const((1, DIFF_WIDTH)),
                  pl.BlockSpec((tm, LANES), lambda i, t: (t, 0)),
                  pl.BlockSpec((tm, LANES), lambda i, t: (t, 0)),
                  const(bd.shape)],
        out_specs=[tok(wd) for wd, _ in outs],
        out_shape=[jax.ShapeDtypeStruct((b, n, wd), dt) for wd, dt in outs],
        compiler_params=pltpu.CompilerParams(dimension_semantics=("arbitrary", "arbitrary"),
                                             vmem_limit_bytes=VMEM_LIMIT),
        name="in_proj_rope" if rope else "in_proj_ctx",
    )(x, sh, sc, g1, w, qg, kg, cos_t, sin_t, bd)


def _gla_direction(qk, v, down, gu, bias, tri, st, head_lane, state_mask, tri_mask, upper):
    c = qk.shape[0]
    logits = _dot(down, gu, HIGHEST) + bias
    la = (jnp.minimum(logits, 0.0) - jnp.log(1.0 + jnp.exp(-jnp.abs(logits)))) * (1.0 / GLA_GATE_NORM)
    cum = _dot(tri, la, HIGHEST)
    tot = cum[0:1] if upper else cum[c - 1:c]
    mid = cum[c // 2:c // 2 + 1]
    q = qk[:, 0:GLA_QK].astype(F32)
    k = qk[:, GLA_QK:2 * GLA_QK].astype(F32)
    q_in = (q * jnp.exp(cum)).astype(BF16)
    q_mid = q * jnp.exp(cum - mid)
    k_mid = (k * jnp.exp(mid - cum)).astype(BF16)
    k_end = (k * jnp.exp(tot - cum)).astype(BF16)
    o_inter = _dot_nt(q_in, st.astype(BF16))
    outs = []
    for hd in range(GLA_HEADS):
        qh = jnp.where(head_lane == hd, q_mid, 0.0).astype(BF16)
        a = jnp.where(tri_mask, _dot_nt(qh, k_mid), 0.0).astype(BF16)
        outs.append(_dot(a, v[:, hd * GLA_DV:(hd + 1) * GLA_DV]))
    o = jnp.concatenate(outs, axis=1) + o_inter
    ut = _dot_tn(v, k_end)
    st_new = jnp.exp(tot) * st + jnp.where(state_mask, ut, 0.0)
    return o, st_new


def _gla_kernel(qk_f, v_f, dn_f, qk_b, v_b, dn_b, gu_ref, bias_ref, s0_ref,
                of_ref, ob_ref, sout_ref, st_ref):
    i = pl.program_id(1)
    c = qk_f.shape[1]

    @pl.when(i == 0)
    def _():
        st_ref[...] = s0_ref[0]

    r = lax.broadcasted_iota(jnp.int32, (c, c), 0)
    cc = lax.broadcasted_iota(jnp.int32, (c, c), 1)
    lower = r >= cc
    upper = r <= cc
    head_lane = lax.broadcasted_iota(jnp.int32, (c, GLA_QK), 1) // GLA_DK
    sr = lax.broadcasted_iota(jnp.int32, (GLA_WIDTH, GLA_QK), 0) // GLA_DV
    sl = lax.broadcasted_iota(jnp.int32, (GLA_WIDTH, GLA_QK), 1) // GLA_DK
    state_mask = sr == sl

    o, st = _gla_direction(qk_f[0], v_f[0], dn_f[0], gu_ref[0], bias_ref[0], lower.astype(F32),
                           st_ref[0], head_lane, state_mask, lower, False)
    of_ref[0] = o
    st_ref[0] = st
    o, st = _gla_direction(qk_b[0], v_b[0], dn_b[0], gu_ref[1], bias_ref[1], upper.astype(F32),
                           st_ref[1], head_lane, state_mask, upper, True)
    ob_ref[0] = o
    st_ref[1] = st

    @pl.when(i == pl.num_programs(1) - 1)
    def _():
        sout_ref[0] = st_ref[...]


def _gla_call(gqk, gv, down, gu, bias, s0):
    b, n, _ = gqk.shape
    c = GLA_CHUNK
    nc = n // c
    fwd = lambda width: pl.BlockSpec((1, c, width), lambda bi, i: (bi, i, 0))
    bwd = lambda width: pl.BlockSpec((1, c, width), lambda bi, i: (bi, nc - 1 - i, 0))
    state = pl.BlockSpec((1, 2, GLA_WIDTH, GLA_QK), lambda bi, i: (bi, 0, 0, 0))
    return pl.pallas_call(
        _gla_kernel,
        grid=(b, nc),
        in_specs=[fwd(2 * GLA_QK), fwd(GLA_WIDTH), fwd(DOWN_PAD),
                  bwd(2 * GLA_QK), bwd(GLA_WIDTH), bwd(DOWN_PAD),
                  pl.BlockSpec(gu.shape, lambda bi, i: (0, 0, 0)),
                  pl.BlockSpec(bias.shape, lambda bi, i: (0, 0, 0)),
                  state],
        out_specs=[fwd(GLA_WIDTH), bwd(GLA_WIDTH), state],
        out_shape=[jax.ShapeDtypeStruct((b, n, GLA_WIDTH), F32),
                   jax.ShapeDtypeStruct((b, n, GLA_WIDTH), F32),
                   jax.ShapeDtypeStruct((b, 2, GLA_WIDTH, GLA_QK), F32)],
        scratch_shapes=[pltpu.VMEM((2, GLA_WIDTH, GLA_QK), F32)],
        compiler_params=pltpu.CompilerParams(dimension_semantics=("arbitrary", "arbitrary"),
                                             vmem_limit_bytes=VMEM_LIMIT),
        name="gla_bidir",
    )(gqk, gv, down, gqk, gv, down, gu, bias, s0)


def _attn_kernel(lq_ref, lk_ref, g_ref, qt_ref, kc_ref, kl_ref, vtc_ref, vtl_ref, o_ref,
                 m_ref, l_ref, acc_ref, *, lam_init):
    tq = qt_ref.shape[2]
    qt = qt_ref[0]
    row = lax.broadcasted_iota(jnp.int32, qt.shape, 0)
    zero = jnp.zeros_like(qt)
    q_bd = jnp.concatenate([jnp.where(row < DIFF_DH, qt, zero),
                            jnp.where(row >= DIFF_DH, qt, zero)], axis=1)

    def scores(k):
        return _dot(k, q_bd)

    s = scores(kc_ref[0])
    m0 = jnp.max(s, axis=0, keepdims=True)
    p = jnp.exp(s - m0)
    m_ref[...] = m0
    l_ref[...] = jnp.sum(p, axis=0, keepdims=True)
    acc_ref[...] = _dot(vtc_ref[0], p.astype(BF16))

    tk = vtl_ref.shape[3]

    def body(j, carry):
        s = scores(kl_ref[0, pl.ds(pl.multiple_of(j * tk, tk), tk), :])
        m_old = m_ref[...]
        m_new = jnp.maximum(m_old, jnp.max(s, axis=0, keepdims=True))
        alpha = jnp.exp(m_old - m_new)
        p = jnp.exp(s - m_new)
        m_ref[...] = m_new
        l_ref[...] = alpha * l_ref[...] + jnp.sum(p, axis=0, keepdims=True)
        acc_ref[...] = alpha * acc_ref[...] + _dot(vtl_ref[0, j], p.astype(BF16))
        return carry

    lax.fori_loop(0, vtl_ref.shape[1], body, 0)

    lqk = jnp.sum(lq_ref[...] * lk_ref[...], axis=1, keepdims=True)
    e = jnp.exp(lqk)
    lam = e[0:1] - e[1:2] + lam_init
    acc = acc_ref[...] * (1.0 / l_ref[...])
    ot = acc[:, 0:tq] - lam * acc[:, tq:2 * tq]
    ms = jnp.mean(ot * ot, axis=0, keepdims=True)
    y = ot * lax.rsqrt(ms + EPS) * (g_ref[...] * (1.0 - lam_init))
    o_ref[0] = y.T.astype(BF16)


def _attn_call(lq, lk, g_col, qt, k_ctx, k_lat, vt_ctx, vt_lat, *, lam_init, tq):
    b, _, n = qt.shape
    n_ctx = k_ctx.shape[1]
    nch, tk = vt_lat.shape[1], vt_lat.shape[3]
    const = lambda shape: pl.BlockSpec(shape, lambda bi, h, qi: tuple(0 for _ in shape))
    return pl.pallas_call(
        functools.partial(_attn_kernel, lam_init=lam_init),
        grid=(b, DIFF_HEADS, n // tq),
        in_specs=[const(lq.shape), const(lk.shape), const(g_col.shape),
                  pl.BlockSpec((1, DIFF_DV, tq), lambda bi, h, qi: (bi, h, qi)),
                  pl.BlockSpec((1, n_ctx, DIFF_DV), lambda bi, h, qi: (bi, 0, h)),
                  pl.BlockSpec((1, n, DIFF_DV), lambda bi, h, qi: (bi, 0, h)),
                  pl.BlockSpec((1, DIFF_DV, n_ctx), lambda bi, h, qi: (bi, h, 0)),
                  pl.BlockSpec((1, nch, DIFF_DV, tk), lambda bi, h, qi: (bi, 0, h, 0))],
        out_specs=pl.BlockSpec((1, tq, DIFF_DV), lambda bi, h, qi: (bi, qi, h)),
        out_shape=jax.ShapeDtypeStruct((b, n, DIFF_WIDTH), BF16),
        scratch_shapes=[pltpu.VMEM((1, 2 * tq), F32), pltpu.VMEM((1, 2 * tq), F32),
                        pltpu.VMEM((DIFF_DV, 2 * tq), F32)],
        compiler_params=pltpu.CompilerParams(
            dimension_semantics=("arbitrary", "arbitrary", "arbitrary"),
            vmem_limit_bytes=VMEM_LIMIT),
        name="diff_attn",
    )(lq, lk, g_col, qt, k_ctx, k_lat, vt_ctx, vt_lat)


def _out_ffn_kernel(x_ref, of_ref, ob_ref, r_ref, od_ref, gt1_ref, sh2_ref, sc2_ref, gt2_ref,
                    gg_ref, g2_ref, wo_ref, wg_ref, wu_ref, wd_ref, o_ref, acc_ref):
    og = of_ref[0] + ob_ref[0]
    r = r_ref[0].astype(F32)
    parts = []
    for hd in range(GLA_HEADS):
        sl = slice(hd * GLA_DV, (hd + 1) * GLA_DV)
        oh = og[:, sl]
        ms = jnp.mean(oh * oh, axis=-1, keepdims=True)
        parts.append((oh * lax.rsqrt(ms + EPS) * gg_ref[...] * _silu(r[:, sl])).astype(BF16))
    gla = jnp.concatenate(parts, axis=1)
    mix = _dot(gla, wo_ref[0:GLA_WIDTH, :]) + _dot(od_ref[0], wo_ref[GLA_WIDTH:, :])
    x1 = x_ref[0] + gt1_ref[0] * mix
    ms = jnp.mean(x1 * x1, axis=-1, keepdims=True)
    h = x1 * lax.rsqrt(ms + EPS) * g2_ref[...]
    hb = (h * (1.0 + sc2_ref[0]) + sh2_ref[0]).astype(BF16)

    acc_ref[...] = jnp.zeros_like(acc_ref)

    def body(j, carry):
        gate = _dot(hb, wg_ref[j])
        up = _dot(hb, wu_ref[j])
        acc_ref[...] += _dot((_silu(gate) * up).astype(BF16), wd_ref[j])
        return carry

    lax.fori_loop(0, wg_ref.shape[0], body, 0)
    o_ref[0] = x1 + gt2_ref[0] * acc_ref[...]


def _out_ffn_call(x, o_f, o_b, gr, od, gt1, sh2, sc2, gt2, gg, g2, wo, wg, wu, wd, *, tm):
    b, n, d = x.shape
    tok = lambda width: pl.BlockSpec((1, tm, width), lambda i, t: (i, t, 0))
    row = pl.BlockSpec((1, 1, d), lambda i, t: (i, 0, 0))
    const = lambda shape: pl.BlockSpec(shape, lambda i, t: tuple(0 for _ in shape),
                                       pipeline_mode=pl.Buffered(1))
    return pl.pallas_call(
        _out_ffn_kernel,
        grid=(b, n // tm),
        in_specs=[tok(d), tok(GLA_WIDTH), tok(GLA_WIDTH), tok(GLA_WIDTH), tok(DIFF_WIDTH),
                  row, row, row, row, const(gg.shape), const(g2.shape),
                  const(wo.shape), const(wg.shape), const(wu.shape), const(wd.shape)],
        out_specs=tok(d),
        out_shape=jax.ShapeDtypeStruct((b, n, d), F32),
        scratch_shapes=[pltpu.VMEM((tm, d), F32)],
        compiler_params=pltpu.CompilerParams(dimension_semantics=("arbitrary", "arbitrary"),
                                             vmem_limit_bytes=VMEM_LIMIT),
        name="out_proj_ffn",
    )(x, o_f, o_b, gr, od, gt1, sh2, sc2, gt2, gg, g2, wo, wg, wu, wd)


def _rope_tables(n):
    t = jnp.arange(n)
    pos = jnp.stack([(t // GRID_W).astype(F32), (t % GRID_W).astype(F32)], axis=1)
    inv_freq = ROPE_BASE ** (-jnp.arange(ROPE_HALF, dtype=F32) / ROPE_HALF)
    ang = pos[:, :, None] * inv_freq
    cos = jnp.cos(ang)[:, :, None, :]
    sin = jnp.sin(ang)[:, :, None, :] * jnp.array([-1.0, 1.0], F32)[None, None, :, None]
    cos = jnp.broadcast_to(cos, (n, 2, 2, ROPE_HALF)).reshape(n, DIFF_DH)
    sin = jnp.broadcast_to(sin, (n, 2, 2, ROPE_HALF)).reshape(n, DIFF_DH)
    return jnp.tile(cos, (1, LANES // DIFF_DH)), jnp.tile(sin, (1, LANES // DIFF_DH))


def _proj_weight(w_in):
    sizes = (GLA_QK, GLA_QK, GLA_WIDTH, GLA_WIDTH, 2 * GLA_GATE_RANK, DIFF_WIDTH, DIFF_WIDTH, DIFF_WIDTH)
    offs = [0]
    for s in sizes:
        offs.append(offs[-1] + s)
    part = lambda i: w_in[:, offs[i]:offs[i + 1]]
    pad = jnp.zeros((w_in.shape[0], DOWN_PAD - 2 * GLA_GATE_RANK), w_in.dtype)
    return jnp.concatenate([part(0), part(1), part(2), part(3), part(5), part(6), part(7), part(4), pad],
                           axis=1).astype(BF16)


def _tile_rows(n, pref):
    for t in pref:
        if n % t == 0:
            return t
    raise ValueError(f"no row tile for {n}")


def kernel(x, c, ctx, c_ctx, w_mod, b_mod, norm1_g, w_in, gla_gate_up, gla_gate_bias, gla_norm_g,
           diff_q_norm_g, diff_k_norm_g, diff_lambda_q, diff_lambda_k, diff_norm_g, w_out, norm2_g,
           w_ffn_in, w_ffn_out):
    assert w_mod.shape[0] == 1, "single-layer stack"
    b, n, d = x.shape
    n_ctx = ctx.shape[1]
    lam_init = 0.8 - 0.6 * math.exp(-0.3 * 0)

    cond = jnp.concatenate([c, c_ctx[None, :], jnp.zeros((-(b + 1) % 8, d), F32)], axis=0)
    mod = _mod_call(cond, w_mod[0], b_mod[0][None, :])
    sh1, sc1, gt1, sh2, sc2, gt2 = [mod[:, i * d:(i + 1) * d] for i in range(6)]
    lat = lambda m: m[0:b, None, :]
    ctx_rows = lambda m: jnp.broadcast_to(m[b:b + 1, None, :], (b, 1, d))

    w_p = _proj_weight(w_in[0])
    g1 = norm1_g[0][None, :]
    qg = jnp.tile(diff_q_norm_g[0], DIFF_WIDTH // DIFF_DH)[None, :]
    kg = jnp.tile(diff_k_norm_g[0], DIFF_WIDTH // DIFF_DH)[None, :]
    gid = jnp.arange(DIFF_WIDTH) // DIFF_DH
    bd = (gid[:, None] == gid[None, :]).astype(BF16)
    cos_t, sin_t = _rope_tables(n)

    tm_c = _tile_rows(n_ctx, (256, 128, 64))
    tm_x = _tile_rows(n, (512, 256, 128, 64))
    pc = _proj_call(ctx, ctx_rows(sh1), ctx_rows(sc1), g1, w_p, qg, kg, cos_t[:n_ctx], sin_t[:n_ctx], bd,
                    rope=False, tm=tm_c)
    px = _proj_call(x, lat(sh1), lat(sc1), g1, w_p, qg, kg, cos_t, sin_t, bd, rope=True, tm=tm_x)
    gqk_c, gv_c, _, down_c, _, dk_c, dv_c = pc
    gqk, gv, gr, down, dq, dk, dv = px

    gu = jnp.zeros((2, DOWN_PAD, GLA_QK), F32)
    for z in range(2):
        gu = gu.at[z, z * GLA_GATE_RANK:(z + 1) * GLA_GATE_RANK, :].set(gla_gate_up[0, z])
    bias = gla_gate_bias[0][:, None, :]
    zero_state = jnp.zeros((b, 2, GLA_WIDTH, GLA_QK), F32)
    _, _, s_ctx = _gla_call(gqk_c, gv_c, down_c, gu, bias, zero_state)
    o_f, o_b, _ = _gla_call(gqk, gv, down, gu, bias, s_ctx)

    tq = _tile_rows(n, (256, 128))
    tk = _tile_rows(n, (512, 256, 128))
    qt = jnp.swapaxes(dq, 1, 2)
    vt_ctx = jnp.swapaxes(dv_c, 1, 2)
    vt_lat = dv.reshape(b, n // tk, tk, DIFF_WIDTH).transpose(0, 1, 3, 2)
    od = _attn_call(diff_lambda_q[0], diff_lambda_k[0], diff_norm_g[0][:, None], qt, dk_c, dk, vt_ctx, vt_lat,
                    lam_init=lam_init, tq=tq)

    fh = w_ffn_out.shape[1]
    fc = 256
    wg = w_ffn_in[0][:, :fh].reshape(d, fh // fc, fc).transpose(1, 0, 2).astype(BF16)
    wu = w_ffn_in[0][:, fh:].reshape(d, fh // fc, fc).transpose(1, 0, 2).astype(BF16)
    wd = w_ffn_out[0].reshape(fh // fc, fc, d).astype(BF16)
    return _out_ffn_call(x, o_f, o_b, gr, od, lat(gt1), lat(sh2), lat(sc2), lat(gt2),
                         gla_norm_g[0][None, :], norm2_g[0][None, :], w_out[0].astype(BF16), wg, wu, wd,
                         tm=tm_x)
```

```python
import functools
import math

import jax
import jax.numpy as jnp
from jax import lax
from jax.experimental import pallas as pl
from jax.experimental.pallas import tpu as pltpu

F32 = jnp.float32
BF16 = jnp.bfloat16
HIGHEST = lax.Precision.HIGHEST

EPS = 1e-6
GRID_W = 64
GLA_HEADS = 4
GLA_DK = 64
GLA_DV = 128
GLA_QK = GLA_HEADS * GLA_DK
GLA_WIDTH = GLA_HEADS * GLA_DV
GLA_GATE_RANK = 16
GLA_GATE_NORM = 16.0
GLA_CHUNK = 64
DIFF_HEADS = 4
DIFF_DH = 64
DIFF_DV = 128
DIFF_WIDTH = DIFF_HEADS * DIFF_DV
ROPE_BASE = 10000.0
ROPE_AXIS_DIM = DIFF_DH // 2
ROPE_HALF = ROPE_AXIS_DIM // 2
LANES = 128
DOWN_PAD = LANES
LOG2_E = math.log2(math.e)
SCORE_BOUND = 75.0

VMEM_LIMIT = 56 * 1024 * 1024


def _dot(a, b, precision=None):
    return jnp.dot(a, b, preferred_element_type=F32, precision=precision)


def _dot_nt(a, b):
    return lax.dot_general(a, b, (((1,), (1,)), ((), ())), preferred_element_type=F32)


def _dot_tn(a, b):
    return lax.dot_general(a, b, (((0,), (0,)), ((), ())), preferred_element_type=F32)


def _silu(x):
    return x / (1.0 + jnp.exp(-x))


def _mod_kernel(c_ref, w_ref, b_ref, o_ref):
    o_ref[...] = _dot(_silu(c_ref[...]), w_ref[...], HIGHEST) + b_ref[...]


def _mod_call(cond, w_mod, b_mod):
    rows, d = cond.shape
    cols = w_mod.shape[1]
    bn = cols // 4
    return pl.pallas_call(
        _mod_kernel,
        grid=(cols // bn,),
        in_specs=[pl.BlockSpec((rows, d), lambda j: (0, 0)),
                  pl.BlockSpec((d, bn), lambda j: (0, j)),
                  pl.BlockSpec((1, bn), lambda j: (0, j))],
        out_specs=pl.BlockSpec((rows, bn), lambda j: (0, j)),
        out_shape=jax.ShapeDtypeStruct((rows, cols), F32),
        compiler_params=pltpu.CompilerParams(dimension_semantics=("arbitrary",),
                                             vmem_limit_bytes=VMEM_LIMIT),
        name="adaln_mod",
    )(cond, w_mod, b_mod)


_C_GQ, _C_GK, _C_GV, _C_GR, _C_DQ, _C_DK, _C_DV, _C_DOWN, _C_END = (
    0, 256, 512, 1024, 1536, 2048, 2560, 3072, 3072 + DOWN_PAD)


def _swap_halves(y):
    lane = lax.broadcasted_iota(jnp.int32, y.shape, 1)
    upper = (lane & ROPE_HALF) != 0
    return jnp.where(upper, pltpu.roll(y, ROPE_HALF, 1), pltpu.roll(y, LANES - ROPE_HALF, 1))


def _proj_kernel(x_ref, sh_ref, sc_ref, g1_ref, w_ref, qg_ref, kg_ref, cos_ref, sin_ref, bd_ref,
                 gqk_o, gv_o, gr_o, down_o, dq_o, dk_o, dv_o, *, rope):
    x = x_ref[0]
    ms = jnp.mean(x * x, axis=-1, keepdims=True)
    h = x * lax.rsqrt(ms + EPS) * g1_ref[...]
    h = h * (1.0 + sc_ref[0]) + sh_ref[0]
    hb = h.astype(BF16)

    def mm(lo, hi):
        return _dot(hb, w_ref[:, lo:hi])

    gqk_o[0, :, 0:GLA_QK] = (mm(_C_GQ, _C_GK) * (GLA_DK ** -0.5)).astype(BF16)
    gqk_o[0, :, GLA_QK:2 * GLA_QK] = mm(_C_GK, _C_GV).astype(BF16)
    gv_o[0] = mm(_C_GV, _C_GR).astype(BF16)
    gr_o[0] = mm(_C_GR, _C_DQ).astype(BF16)
    dv_o[0] = mm(_C_DV, _C_DOWN).astype(BF16)
    down_o[0] = mm(_C_DOWN, _C_END)

    def qk_norm(y, g_ref, o_ref, scale):
        ss = _dot((y * y).astype(BF16), bd_ref[...])
        yn = y * lax.rsqrt(ss * (1.0 / DIFF_DH) + EPS) * g_ref[...]
        for s in range(DIFF_WIDTH // LANES):
            slab = yn[:, s * LANES:(s + 1) * LANES]
            if rope:
                slab = slab * cos_ref[...] + _swap_halves(slab) * sin_ref[...]
            o_ref[0, :, s * LANES:(s + 1) * LANES] = (slab * scale).astype(BF16)

    qk_norm(mm(_C_DQ, _C_DK), qg_ref, dq_o, DIFF_DH ** -0.5 * LOG2_E)
    qk_norm(mm(_C_DK, _C_DV), kg_ref, dk_o, 1.0)


def _proj_call(x, sh, sc, g1, w, qg, kg, cos_t, sin_t, bd, *, rope, tm):
    b, n, d = x.shape
    assert n % tm == 0
    tok = lambda width: pl.BlockSpec((1, tm, width), lambda i, t: (i, t, 0))
    const = lambda shape: pl.BlockSpec(shape, lambda i, t: tuple(0 for _ in shape))
    row = pl.BlockSpec((1, 1, d), lambda i, t: (i, 0, 0))
    outs = [(2 * GLA_QK, BF16), (GLA_WIDTH, BF16), (GLA_WIDTH, BF16), (DOWN_PAD, F32),
            (DIFF_WIDTH, BF16), (DIFF_WIDTH, BF16), (DIFF_WIDTH, BF16)]
    return pl.pallas_call(
        functools.partial(_proj_kernel, rope=rope),
        grid=(b, n // tm),
        in_specs=[tok(d), row, row, const((1, d)), const(w.shape), const((1, DIFF_WIDTH)),
                  const((1, DIFF_WIDTH)),
                  pl.BlockSpec((tm, LANES), lambda i, t: (t, 0)),
                  pl.BlockSpec((tm, LANES), lambda i, t: (t, 0)),
                  const(bd.shape)],
        out_specs=[tok(wd) for wd, _ in outs],
        out_shape=[jax.ShapeDtypeStruct((b, n, wd), dt) for wd, dt in outs],
        compiler_params=pltpu.CompilerParams(dimension_semantics=("arbitrary", "arbitrary"),
                                             vmem_limit_bytes=VMEM_LIMIT),
        name="in_proj_rope" if rope else "in_proj_ctx",
    )(x, sh, sc, g1, w, qg, kg, cos_t, sin_t, bd)


def _gla_direction(qk, v, down, gu, bias, tri, st, head_lane, state_mask, tri_mask, upper):
    c = qk.shape[0]
    logits = _dot(down, gu, HIGHEST) + bias
    la = (jnp.minimum(logits, 0.0) - jnp.log(1.0 + jnp.exp(-jnp.abs(logits)))) * (1.0 / GLA_GATE_NORM)
    cum = _dot(tri, la, HIGHEST)
    tot = cum[0:1] if upper else cum[c - 1:c]
    mid = cum[c // 2:c // 2 + 1]
    q = qk[:, 0:GLA_QK].astype(F32)
    k = qk[:, GLA_QK:2 * GLA_QK].astype(F32)
    q_in = (q * jnp.exp(cum)).astype(BF16)
    q_mid = q * jnp.exp(cum - mid)
    k_mid = (k * jnp.exp(mid - cum)).astype(BF16)
    k_end = (k * jnp.exp(tot - cum)).astype(BF16)
    o_inter = _dot_nt(q_in, st.astype(BF16))
    outs = []
    for hd in range(GLA_HEADS):
        qh = jnp.where(head_lane == hd, q_mid, 0.0).astype(BF16)
        a = jnp.where(tri_mask, _dot_nt(qh, k_mid), 0.0).astype(BF16)
        outs.append(_dot(a, v[:, hd * GLA_DV:(hd + 1) * GLA_DV]))
    o = jnp.concatenate(outs, axis=1) + o_inter
    ut = _dot_tn(v, k_end)
    st_new = jnp.exp(tot) * st + jnp.where(state_mask, ut, 0.0)
    return o, st_new


def _gla_kernel(qk_f, v_f, dn_f, qk_b, v_b, dn_b, gu_ref, bias_ref, s0_ref,
                of_ref, ob_ref, sout_ref, st_ref):
    i = pl.program_id(1)
    c = qk_f.shape[1]

    @pl.when(i == 0)
    def _():
        st_ref[...] = s0_ref[0]

    r = lax.broadcasted_iota(jnp.int32, (c, c), 0)
    cc = lax.broadcasted_iota(jnp.int32, (c, c), 1)
    lower = r >= cc
    upper = r <= cc
    head_lane = lax.broadcasted_iota(jnp.int32, (c, GLA_QK), 1) // GLA_DK
    sr = lax.broadcasted_iota(jnp.int32, (GLA_WIDTH, GLA_QK), 0) // GLA_DV
    sl = lax.broadcasted_iota(jnp.int32, (GLA_WIDTH, GLA_QK), 1) // GLA_DK
    state_mask = sr == sl

    o, st = _gla_direction(qk_f[0], v_f[0], dn_f[0], gu_ref[0], bias_ref[0], lower.astype(F32),
                           st_ref[0], head_lane, state_mask, lower, False)
    of_ref[0] = o
    st_ref[0] = st
    o, st = _gla_direction(qk_b[0], v_b[0], dn_b[0], gu_ref[1], bias_ref[1], upper.astype(F32),
                           st_ref[1], head_lane, state_mask, upper, True)
    ob_ref[0] = o
    st_ref[1] = st

    @pl.when(i == pl.num_programs(1) - 1)
    def _():
        sout_ref[0] = st_ref[...]


def _gla_call(gqk, gv, down, gu, bias, s0):
    b, n, _ = gqk.shape
    c = GLA_CHUNK
    nc = n // c
    fwd = lambda width: pl.BlockSpec((1, c, width), lambda bi, i: (bi, i, 0))
    bwd = lambda width: pl.BlockSpec((1, c, width), lambda bi, i: (bi, nc - 1 - i, 0))
    state = pl.BlockSpec((1, 2, GLA_WIDTH, GLA_QK), lambda bi, i: (bi, 0, 0, 0))
    return pl.pallas_call(
        _gla_kernel,
        grid=(b, nc),
        in_specs=[fwd(2 * GLA_QK), fwd(GLA_WIDTH), fwd(DOWN_PAD),
                  bwd(2 * GLA_QK), bwd(GLA_WIDTH), bwd(DOWN_PAD),
                  pl.BlockSpec(gu.shape, lambda bi, i: (0, 0, 0)),
                  pl.BlockSpec(bias.shape, lambda bi, i: (0, 0, 0)),
                  state],
        out_specs=[fwd(GLA_WIDTH), bwd(GLA_WIDTH), state],
        out_shape=[jax.ShapeDtypeStruct((b, n, GLA_WIDTH), F32),
                   jax.ShapeDtypeStruct((b, n, GLA_WIDTH), F32),
                   jax.ShapeDtypeStruct((b, 2, GLA_WIDTH, GLA_QK), F32)],
        scratch_shapes=[pltpu.VMEM((2, GLA_WIDTH, GLA_QK), F32)],
        compiler_params=pltpu.CompilerParams(dimension_semantics=("arbitrary", "arbitrary"),
                                             vmem_limit_bytes=VMEM_LIMIT),
        name="gla_bidir",
    )(gqk, gv, down, gqk, gv, down, gu, bias, s0)


def _block_diag_queries(qt):
    row = lax.broadcasted_iota(jnp.int32, qt.shape, 0)
    zero = jnp.zeros_like(qt)
    return jnp.concatenate([jnp.where(row < DIFF_DH, qt, zero),
                            jnp.where(row >= DIFF_DH, qt, zero)], axis=1)


def _attn_finish(lq_ref, lk_ref, g_ref, o_ref, acc, l, tq, lam_init):
    lqk = jnp.sum(lq_ref[...] * lk_ref[...], axis=1, keepdims=True)
    e = jnp.exp(lqk)
    lam = e[0:1] - e[1:2] + lam_init
    acc = acc * (1.0 / l)
    ot = acc[:, 0:tq] - lam * acc[:, tq:2 * tq]
    ms = jnp.mean(ot * ot, axis=0, keepdims=True)
    y = ot * lax.rsqrt(ms + EPS) * (g_ref[...] * (1.0 - lam_init))
    o_ref[0] = y.T.astype(BF16)


def _attn_online_kernel(lq_ref, lk_ref, g_ref, qt_ref, kc_ref, kl_ref, vtc_ref, vtl_ref, o_ref,
                        m_ref, l_ref, acc_ref, *, lam_init):
    tq = qt_ref.shape[2]
    q_bd = _block_diag_queries(qt_ref[0])

    s = _dot(kc_ref[0], q_bd)
    m0 = jnp.max(s, axis=0, keepdims=True)
    p = jnp.exp2(s - m0)
    m_ref[...] = m0
    l_ref[...] = jnp.sum(p, axis=0, keepdims=True)
    acc_ref[...] = _dot(vtc_ref[0], p.astype(BF16))

    tk = vtl_ref.shape[3]

    def body(j, carry):
        s = _dot(kl_ref[0, pl.ds(pl.multiple_of(j * tk, tk), tk), :], q_bd)
        m_old = m_ref[...]
        m_new = jnp.maximum(m_old, jnp.max(s, axis=0, keepdims=True))
        alpha = jnp.exp2(m_old - m_new)
        p = jnp.exp2(s - m_new)
        m_ref[...] = m_new
        l_ref[...] = alpha * l_ref[...] + jnp.sum(p, axis=0, keepdims=True)
        acc_ref[...] = alpha * acc_ref[...] + _dot(vtl_ref[0, j], p.astype(BF16))
        return carry

    lax.fori_loop(0, vtl_ref.shape[1], body, 0)
    _attn_finish(lq_ref, lk_ref, g_ref, o_ref, acc_ref[...], l_ref[...], tq, lam_init)


def _attn_bounded_kernel(lq_ref, lk_ref, g_ref, qt_ref, kc_ref, kl_ref, vtc_ref, vtl_ref, o_ref,
                         s_ref, *, lam_init):
    tq = qt_ref.shape[2]
    tk = vtl_ref.shape[3]
    q_bd = _block_diag_queries(qt_ref[0])
    n_ctx = kc_ref.shape[1]
    tiles = [(n_ctx, lambda: kc_ref[0], lambda: vtc_ref[0])]
    for j in range(vtl_ref.shape[1]):
        tiles.append((tk, lambda j=j: kl_ref[0, j * tk:(j + 1) * tk, :], lambda j=j: vtl_ref[0, j]))

    s_ref[0, 0:n_ctx, :] = _dot(tiles[0][1](), q_bd)
    l = jnp.zeros((1, 2 * tq), F32)
    acc = jnp.zeros((DIFF_DV, 2 * tq), F32)
    for t, (rows, _, vt) in enumerate(tiles):
        if t + 1 < len(tiles):
            nxt = tiles[t + 1]
            s_ref[(t + 1) % 2, 0:nxt[0], :] = _dot(nxt[1](), q_bd)
        p = jnp.exp2(s_ref[t % 2, 0:rows, :])
        l = l + jnp.sum(p, axis=0, keepdims=True)
        acc = acc + _dot(vt(), p.astype(BF16))
    _attn_finish(lq_ref, lk_ref, g_ref, o_ref, acc, l, tq, lam_init)


def _attn_call(body, lq, lk, g_col, qt, k_ctx, k_lat, vt_ctx, vt_lat, *, lam_init, tq):
    b, _, n = qt.shape
    n_ctx = k_ctx.shape[1]
    nch, tk = vt_lat.shape[1], vt_lat.shape[3]
    const = lambda shape: pl.BlockSpec(shape, lambda bi, h, qi: tuple(0 for _ in shape))
    return pl.pallas_call(
        functools.partial(body, lam_init=lam_init),
        grid=(b, DIFF_HEADS, n // tq),
        in_specs=[const(lq.shape), const(lk.shape), const(g_col.shape),
                  pl.BlockSpec((1, DIFF_DV, tq), lambda bi, h, qi: (bi, h, qi)),
                  pl.BlockSpec((1, n_ctx, DIFF_DV), lambda bi, h, qi: (bi, 0, h)),
                  pl.BlockSpec((1, n, DIFF_DV), lambda bi, h, qi: (bi, 0, h)),
                  pl.BlockSpec((1, DIFF_DV, n_ctx), lambda bi, h, qi: (bi, h, 0)),
                  pl.BlockSpec((1, nch, DIFF_DV, tk), lambda bi, h, qi: (bi, 0, h, 0))],
        out_specs=pl.BlockSpec((1, tq, DIFF_DV), lambda bi, h, qi: (bi, qi, h)),
        out_shape=jax.ShapeDtypeStruct((b, n, DIFF_WIDTH), BF16),
        scratch_shapes=(
            [pltpu.VMEM((2, max(tk, n_ctx), 2 * tq), F32)] if body is _attn_bounded_kernel else
            [pltpu.VMEM((1, 2 * tq), F32), pltpu.VMEM((1, 2 * tq), F32), pltpu.VMEM((DIFF_DV, 2 * tq), F32)]),
        compiler_params=pltpu.CompilerParams(
            dimension_semantics=("arbitrary", "arbitrary", "arbitrary"),
            vmem_limit_bytes=VMEM_LIMIT),
        name=body.__name__.strip("_"),
    )(lq, lk, g_col, qt, k_ctx, k_lat, vt_ctx, vt_lat)


def _out_ffn_kernel(x_ref, of_ref, ob_ref, r_ref, od_ref, gt1_ref, sh2_ref, sc2_ref, gt2_ref,
                    gg_ref, g2_ref, wo_ref, wg_ref, wu_ref, wd_ref, o_ref, acc_ref):
    og = of_ref[0] + ob_ref[0]
    r = r_ref[0].astype(F32)
    parts = []
    for hd in range(GLA_HEADS):
        sl = slice(hd * GLA_DV, (hd + 1) * GLA_DV)
        oh = og[:, sl]
        ms = jnp.mean(oh * oh, axis=-1, keepdims=True)
        parts.append((oh * lax.rsqrt(ms + EPS) * gg_ref[...] * _silu(r[:, sl])).astype(BF16))
    gla = jnp.concatenate(parts, axis=1)
    mix = _dot(gla, wo_ref[0:GLA_WIDTH, :]) + _dot(od_ref[0], wo_ref[GLA_WIDTH:, :])
    x1 = x_ref[0] + gt1_ref[0] * mix
    ms = jnp.mean(x1 * x1, axis=-1, keepdims=True)
    h = x1 * lax.rsqrt(ms + EPS) * g2_ref[...]
    hb = (h * (1.0 + sc2_ref[0]) + sh2_ref[0]).astype(BF16)

    acc_ref[...] = jnp.zeros_like(acc_ref)

    def body(j, carry):
        gate = _dot(hb, wg_ref[j])
        up = _dot(hb, wu_ref[j])
        acc_ref[...] += _dot((_silu(gate) * up).astype(BF16), wd_ref[j])
        return carry

    lax.fori_loop(0, wg_ref.shape[0], body, 0)
    o_ref[0] = x1 + gt2_ref[0] * acc_ref[...]


def _out_ffn_call(x, o_f, o_b, gr, od, gt1, sh2, sc2, gt2, gg, g2, wo, wg, wu, wd, *, tm):
    b, n, d = x.shape
    tok = lambda width: pl.BlockSpec((1, tm, width), lambda i, t: (i, t, 0))
    row = pl.BlockSpec((1, 1, d), lambda i, t: (i, 0, 0))
    const = lambda shape: pl.BlockSpec(shape, lambda i, t: tuple(0 for _ in shape),
                                       pipeline_mode=pl.Buffered(1))
    return pl.pallas_call(
        _out_ffn_kernel,
        grid=(b, n // tm),
        in_specs=[tok(d), tok(GLA_WIDTH), tok(GLA_WIDTH), tok(GLA_WIDTH), tok(DIFF_WIDTH),
                  row, row, row, row, const(gg.shape), const(g2.shape),
                  const(wo.shape), const(wg.shape), const(wu.shape), const(wd.shape)],
        out_specs=tok(d),
        out_shape=jax.ShapeDtypeStruct((b, n, d), F32),
        scratch_shapes=[pltpu.VMEM((tm, d), F32)],
        compiler_params=pltpu.CompilerParams(dimension_semantics=("arbitrary", "arbitrary"),
                                             vmem_limit_bytes=VMEM_LIMIT),
        name="out_proj_ffn",
    )(x, o_f, o_b, gr, od, gt1, sh2, sc2, gt2, gg, g2, wo, wg, wu, wd)


def _rope_tables(n):
    t = jnp.arange(n)
    pos = jnp.stack([(t // GRID_W).astype(F32), (t % GRID_W).astype(F32)], axis=1)
    inv_freq = ROPE_BASE ** (-jnp.arange(ROPE_HALF, dtype=F32) / ROPE_HALF)
    ang = pos[:, :, None] * inv_freq
    cos = jnp.cos(ang)[:, :, None, :]
    sin = jnp.sin(ang)[:, :, None, :] * jnp.array([-1.0, 1.0], F32)[None, None, :, None]
    cos = jnp.broadcast_to(cos, (n, 2, 2, ROPE_HALF)).reshape(n, DIFF_DH)
    sin = jnp.broadcast_to(sin, (n, 2, 2, ROPE_HALF)).reshape(n, DIFF_DH)
    return jnp.tile(cos, (1, LANES // DIFF_DH)), jnp.tile(sin, (1, LANES // DIFF_DH))


def _proj_weight(w_in):
    sizes = (GLA_QK, GLA_QK, GLA_WIDTH, GLA_WIDTH, 2 * GLA_GATE_RANK, DIFF_WIDTH, DIFF_WIDTH, DIFF_WIDTH)
    offs = [0]
    for s in sizes:
        offs.append(offs[-1] + s)
    part = lambda i: w_in[:, offs[i]:offs[i + 1]]
    pad = jnp.zeros((w_in.shape[0], DOWN_PAD - 2 * GLA_GATE_RANK), w_in.dtype)
    return jnp.concatenate([part(0), part(1), part(2), part(3), part(5), part(6), part(7), part(4), pad],
                           axis=1).astype(BF16)


def _tile_rows(n, pref):
    for t in pref:
        if n % t == 0:
            return t
    raise ValueError(f"no row tile for {n}")


def kernel(x, c, ctx, c_ctx, w_mod, b_mod, norm1_g, w_in, gla_gate_up, gla_gate_bias, gla_norm_g,
           diff_q_norm_g, diff_k_norm_g, diff_lambda_q, diff_lambda_k, diff_norm_g, w_out, norm2_g,
           w_ffn_in, w_ffn_out):
    assert w_mod.shape[0] == 1, "single-layer stack"
    b, n, d = x.shape
    n_ctx = ctx.shape[1]
    lam_init = 0.8 - 0.6 * math.exp(-0.3 * 0)

    cond = jnp.concatenate([c, c_ctx[None, :], jnp.zeros((-(b + 1) % 8, d), F32)], axis=0)
    mod = _mod_call(cond, w_mod[0], b_mod[0][None, :])
    sh1, sc1, gt1, sh2, sc2, gt2 = [mod[:, i * d:(i + 1) * d] for i in range(6)]
    lat = lambda m: m[0:b, None, :]
    ctx_rows = lambda m: jnp.broadcast_to(m[b:b + 1, None, :], (b, 1, d))

    w_p = _proj_weight(w_in[0])
    g1 = norm1_g[0][None, :]
    qg = jnp.tile(diff_q_norm_g[0], DIFF_WIDTH // DIFF_DH)[None, :]
    kg = jnp.tile(diff_k_norm_g[0], DIFF_WIDTH // DIFF_DH)[None, :]
    gid = jnp.arange(DIFF_WIDTH) // DIFF_DH
    bd = (gid[:, None] == gid[None, :]).astype(BF16)
    cos_t, sin_t = _rope_tables(n)

    tm_c = _tile_rows(n_ctx, (256, 128, 64))
    tm_x = _tile_rows(n, (512, 256, 128, 64))
    pc = _proj_call(ctx, ctx_rows(sh1), ctx_rows(sc1), g1, w_p, qg, kg, cos_t[:n_ctx], sin_t[:n_ctx], bd,
                    rope=False, tm=tm_c)
    px = _proj_call(x, lat(sh1), lat(sc1), g1, w_p, qg, kg, cos_t, sin_t, bd, rope=True, tm=tm_x)
    gqk_c, gv_c, _, down_c, _, dk_c, dv_c = pc
    gqk, gv, gr, down, dq, dk, dv = px

    gu = jnp.zeros((2, DOWN_PAD, GLA_QK), F32)
    for z in range(2):
        gu = gu.at[z, z * GLA_GATE_RANK:(z + 1) * GLA_GATE_RANK, :].set(gla_gate_up[0, z])
    bias = gla_gate_bias[0][:, None, :]
    zero_state = jnp.zeros((b, 2, GLA_WIDTH, GLA_QK), F32)
    _, _, s_ctx = _gla_call(gqk_c, gv_c, down_c, gu, bias, zero_state)
    o_f, o_b, _ = _gla_call(gqk, gv, down, gu, bias, s_ctx)

    tq = _tile_rows(n, (256, 128))
    tk = _tile_rows(n, (512, 256, 128))
    qt = jnp.swapaxes(dq, 1, 2)
    vt_ctx = jnp.swapaxes(dv_c, 1, 2)
    vt_lat = dv.reshape(b, n // tk, tk, DIFF_WIDTH).transpose(0, 1, 3, 2)
    score_bound = 8.0 * jnp.max(jnp.abs(diff_q_norm_g[0])) * jnp.max(jnp.abs(diff_k_norm_g[0]))
    attn_args = (diff_lambda_q[0], diff_lambda_k[0], diff_norm_g[0][:, None], qt, dk_c, dk, vt_ctx, vt_lat)
    od = lax.cond(
        score_bound <= SCORE_BOUND,
        lambda *a: _attn_call(_attn_bounded_kernel, *a, lam_init=lam_init, tq=tq),
        lambda *a: _attn_call(_attn_online_kernel, *a, lam_init=lam_init, tq=tq),
        *attn_args)

    fh = w_ffn_out.shape[1]
    fc = 256
    wg = w_ffn_in[0][:, :fh].reshape(d, fh // fc, fc).transpose(1, 0, 2).astype(BF16)
    wu = w_ffn_in[0][:, fh:].reshape(d, fh // fc, fc).transpose(1, 0, 2).astype(BF16)
    wd = w_ffn_out[0].reshape(fh // fc, fc, d).astype(BF16)
    return _out_ffn_call(x, o_f, o_b, gr, od, lat(gt1), lat(sh2), lat(sc2), lat(gt2),
                         gla_norm_g[0][None, :], norm2_g[0][None, :], w_out[0].astype(BF16), wg, wu, wd,
                         tm=tm_x)
```

```python
import functools
import math

import jax
import jax.numpy as jnp
from jax import lax
from jax.experimental import pallas as pl
from jax.experimental.pallas import tpu as pltpu

F32 = jnp.float32
BF16 = jnp.bfloat16
HIGHEST = lax.Precision.HIGHEST

EPS = 1e-6
GRID_W = 64
GLA_HEADS = 4
GLA_DK = 64
GLA_DV = 128
GLA_QK = GLA_HEADS * GLA_DK
GLA_WIDTH = GLA_HEADS * GLA_DV
GLA_GATE_RANK = 16
GLA_GATE_NORM = 16.0
GLA_CHUNK = 64
DIFF_HEADS = 4
DIFF_DH = 64
DIFF_DV = 128
DIFF_WIDTH = DIFF_HEADS * DIFF_DV
ROPE_BASE = 10000.0
ROPE_AXIS_DIM = DIFF_DH // 2
ROPE_HALF = ROPE_AXIS_DIM // 2
LANES = 128
DOWN_PAD = LANES
LOG2_E = math.log2(math.e)
SCORE_BOUND = 75.0

VMEM_LIMIT = 56 * 1024 * 1024


def _dot(a, b, precision=None):
    return jnp.dot(a, b, preferred_element_type=F32, precision=precision)


def _dot_nt(a, b):
    return lax.dot_general(a, b, (((1,), (1,)), ((), ())), preferred_element_type=F32)


def _dot_tn(a, b):
    return lax.dot_general(a, b, (((0,), (0,)), ((), ())), preferred_element_type=F32)


def _silu(x):
    return x / (1.0 + jnp.exp(-x))


def _mod_kernel(c_ref, w_ref, b_ref, o_ref):
    o_ref[...] = _dot(_silu(c_ref[...]), w_ref[...], HIGHEST) + b_ref[...]


def _mod_call(cond, w_mod, b_mod):
    rows, d = cond.shape
    cols = w_mod.shape[1]
    bn = cols // 4
    return pl.pallas_call(
        _mod_kernel,
        grid=(cols // bn,),
        in_specs=[pl.BlockSpec((rows, d), lambda j: (0, 0)),
                  pl.BlockSpec((d, bn), lambda j: (0, j)),
                  pl.BlockSpec((1, bn), lambda j: (0, j))],
        out_specs=pl.BlockSpec((rows, bn), lambda j: (0, j)),
        out_shape=jax.ShapeDtypeStruct((rows, cols), F32),
        compiler_params=pltpu.CompilerParams(dimension_semantics=("arbitrary",),
                                             vmem_limit_bytes=VMEM_LIMIT),
        name="adaln_mod",
    )(cond, w_mod, b_mod)


_C_GQ, _C_GK, _C_GV, _C_GR, _C_DQ, _C_DK, _C_DV, _C_DOWN, _C_END = (
    0, 256, 512, 1024, 1536, 2048, 2560, 3072, 3072 + DOWN_PAD)


def _swap_halves(y):
    lane = lax.broadcasted_iota(jnp.int32, y.shape, 1)
    upper = (lane & ROPE_HALF) != 0
    return jnp.where(upper, pltpu.roll(y, ROPE_HALF, 1), pltpu.roll(y, LANES - ROPE_HALF, 1))


def _proj_kernel(x_ref, sh_ref, sc_ref, g1_ref, w_ref, qg_ref, kg_ref, cos_ref, sin_ref, bd_ref,
                 gqk_o, gv_o, gr_o, down_o, dq_o, dk_o, dv_o, *, rope):
    x = x_ref[0]
    ms = jnp.mean(x * x, axis=-1, keepdims=True)
    h = x * lax.rsqrt(ms + EPS) * g1_ref[...]
    h = h * (1.0 + sc_ref[0]) + sh_ref[0]
    hb = h.astype(BF16)

    def mm(lo, hi):
        return _dot(hb, w_ref[:, lo:hi])

    gqk_o[0, :, 0:GLA_QK] = (mm(_C_GQ, _C_GK) * (GLA_DK ** -0.5)).astype(BF16)
    gqk_o[0, :, GLA_QK:2 * GLA_QK] = mm(_C_GK, _C_GV).astype(BF16)
    gv_o[0] = mm(_C_GV, _C_GR).astype(BF16)
    gr_o[0] = mm(_C_GR, _C_DQ).astype(BF16)
    dv = mm(_C_DV, _C_DOWN)
    for s in range(DIFF_WIDTH // LANES):
        dv_o[0, 0, s * LANES:(s + 1) * LANES, :] = dv[:, s * LANES:(s + 1) * LANES].T.astype(BF16)
    down_o[0] = mm(_C_DOWN, _C_END)

    def qk_norm(y, g_ref, scale, store):
        ss = _dot((y * y).astype(BF16), bd_ref[...])
        yn = y * lax.rsqrt(ss * (1.0 / DIFF_DH) + EPS) * g_ref[...]
        for s in range(DIFF_WIDTH // LANES):
            slab = yn[:, s * LANES:(s + 1) * LANES]
            if rope:
                slab = slab * cos_ref[...] + _swap_halves(slab) * sin_ref[...]
            store(s, slab * scale)

    def store_q(s, slab):
        dq_o[0, s * LANES:(s + 1) * LANES, :] = slab.T.astype(BF16)

    def store_k(s, slab):
        dk_o[0, :, s * LANES:(s + 1) * LANES] = slab.astype(BF16)

    qk_norm(mm(_C_DQ, _C_DK), qg_ref, DIFF_DH ** -0.5 * LOG2_E, store_q)
    qk_norm(mm(_C_DK, _C_DV), kg_ref, 1.0, store_k)


def _proj_call(x, sh, sc, g1, w, qg, kg, cos_t, sin_t, bd, *, rope, tm):
    b, n, d = x.shape
    assert n % tm == 0
    tok = lambda width: pl.BlockSpec((1, tm, width), lambda i, t: (i, t, 0))
    const = lambda shape: pl.BlockSpec(shape, lambda i, t: tuple(0 for _ in shape))
    row = pl.BlockSpec((1, 1, d), lambda i, t: (i, 0, 0))
    outs = [(2 * GLA_QK, BF16), (GLA_WIDTH, BF16), (GLA_WIDTH, BF16), (DOWN_PAD, F32), (DIFF_WIDTH, BF16)]
    out_specs = [tok(wd) for wd, _ in outs]
    out_shape = [jax.ShapeDtypeStruct((b, n, wd), dt) for wd, dt in outs]
    out_specs.insert(4, pl.BlockSpec((1, DIFF_WIDTH, tm), lambda i, t: (i, 0, t)))
    out_shape.insert(4, jax.ShapeDtypeStruct((b, DIFF_WIDTH, n), BF16))
    out_specs.append(pl.BlockSpec((1, 1, DIFF_WIDTH, tm), lambda i, t: (i, t, 0, 0)))
    out_shape.append(jax.ShapeDtypeStruct((b, n // tm, DIFF_WIDTH, tm), BF16))
    return pl.pallas_call(
        functools.partial(_proj_kernel, rope=rope),
        grid=(b, n // tm),
        in_specs=[tok(d), row, row, const((1, d)), const(w.shape), const((1, DIFF_WIDTH)),
                  const((1, DIFF_WIDTH)),
                  pl.BlockSpec((tm, LANES), lambda i, t: (t, 0)),
                  pl.BlockSpec((tm, LANES), lambda i, t: (t, 0)),
                  const(bd.shape)],
        out_specs=out_specs,
        out_shape=out_shape,
        compiler_params=pltpu.CompilerParams(dimension_semantics=("arbitrary", "arbitrary"),
                                             vmem_limit_bytes=VMEM_LIMIT),
        name="in_proj_rope" if rope else "in_proj_ctx",
    )(x, sh, sc, g1, w, qg, kg, cos_t, sin_t, bd)


def _split_bf16(x):
    hi = x.astype(BF16)
    return hi, (x - hi.astype(F32)).astype(BF16)


def _gla_direction(qk_ref, v_ref, dn_ref, gu, bias, tri, st_ref, z, o_ref, head_lane, state_mask, tri_mask,
                   upper):
    nb, c = qk_ref.shape[0], qk_ref.shape[1]
    rows = nb * c
    down = dn_ref[...].reshape(rows, DOWN_PAD)
    dh, dl = _split_bf16(down)
    gh, gl = gu
    logits = _dot(dh, gh) + (_dot(dl, gh) + _dot(dh, gl)) + bias
    la = (jnp.minimum(logits, 0.0) - jnp.log(1.0 + jnp.exp(-jnp.abs(logits)))) * (1.0 / GLA_GATE_NORM)
    lh, ll = _split_bf16(la)
    cum_all = _dot(tri, lh) + _dot(tri, ll)

    q_in, q_mid, k_mid, k_end, tots = [], [], [], [], []
    for bi in range(nb):
        cum = cum_all[bi * c:(bi + 1) * c]
        tot = cum[0:1] if upper else cum[c - 1:c]
        mid = cum[c // 2:c // 2 + 1]
        q = qk_ref[bi, :, 0:GLA_QK].astype(F32)
        k = qk_ref[bi, :, GLA_QK:2 * GLA_QK].astype(F32)
        q_in.append((q * jnp.exp(cum)).astype(BF16))
        q_mid.append(q * jnp.exp(cum - mid))
        k_mid.append((k * jnp.exp(mid - cum)).astype(BF16))
        k_end.append((k * jnp.exp(tot - cum)).astype(BF16))
        tots.append(tot)
    q_mid = jnp.concatenate(q_mid, axis=0)
    k_mid = jnp.concatenate(k_mid, axis=0)
    v_all = v_ref[...].reshape(rows, GLA_WIDTH)
    outs = []
    for hd in range(GLA_HEADS):
        qh = jnp.where(head_lane == hd, q_mid, 0.0).astype(BF16)
        a = jnp.where(tri_mask, _dot_nt(qh, k_mid), 0.0).astype(BF16)
        outs.append(_dot(a, v_all[:, hd * GLA_DV:(hd + 1) * GLA_DV]))
    o_intra = jnp.concatenate(outs, axis=1)

    for bi in range(nb):
        st = st_ref[bi, z]
        o_ref[bi] = o_intra[bi * c:(bi + 1) * c] + _dot_nt(q_in[bi], st.astype(BF16))
        ut = _dot_tn(v_ref[bi], k_end[bi])
        st_ref[bi, z] = jnp.exp(tots[bi]) * st + jnp.where(state_mask, ut, 0.0)


def _gla_kernel(qk_f, v_f, dn_f, qk_b, v_b, dn_b, gu_ref, bias_ref, s0_ref,
                of_ref, ob_ref, sout_ref, st_ref):
    i = pl.program_id(0)
    nb, c = qk_f.shape[0], qk_f.shape[1]
    rows = nb * c

    @pl.when(i == 0)
    def _():
        st_ref[...] = s0_ref[...]

    r = lax.broadcasted_iota(jnp.int32, (rows, rows), 0)
    cc = lax.broadcasted_iota(jnp.int32, (rows, rows), 1)
    same = (r // c) == (cc // c)
    lower = same & (r >= cc)
    upper = same & (r <= cc)
    head_lane = lax.broadcasted_iota(jnp.int32, (rows, GLA_QK), 1) // GLA_DK
    sr = lax.broadcasted_iota(jnp.int32, (GLA_WIDTH, GLA_QK), 0) // GLA_DV
    sl = lax.broadcasted_iota(jnp.int32, (GLA_WIDTH, GLA_QK), 1) // GLA_DK
    state_mask = sr == sl

    _gla_direction(qk_f, v_f, dn_f, _split_bf16(gu_ref[0]), bias_ref[0], lower.astype(BF16), st_ref, 0, of_ref,
                   head_lane, state_mask, lower, False)
    _gla_direction(qk_b, v_b, dn_b, _split_bf16(gu_ref[1]), bias_ref[1], upper.astype(BF16), st_ref, 1, ob_ref,
                   head_lane, state_mask, upper, True)

    @pl.when(i == pl.num_programs(0) - 1)
    def _():
        sout_ref[...] = st_ref[...]


def _gla_call(gqk, gv, down, gu, bias, s0):
    b, n, _ = gqk.shape
    c = GLA_CHUNK
    nc = n // c
    fwd = lambda width: pl.BlockSpec((b, c, width), lambda i: (0, i, 0))
    bwd = lambda width: pl.BlockSpec((b, c, width), lambda i: (0, nc - 1 - i, 0))
    state = pl.BlockSpec((b, 2, GLA_WIDTH, GLA_QK), lambda i: (0, 0, 0, 0))
    return pl.pallas_call(
        _gla_kernel,
        grid=(nc,),
        in_specs=[fwd(2 * GLA_QK), fwd(GLA_WIDTH), fwd(DOWN_PAD),
                  bwd(2 * GLA_QK), bwd(GLA_WIDTH), bwd(DOWN_PAD),
                  pl.BlockSpec(gu.shape, lambda i: (0, 0, 0)),
                  pl.BlockSpec(bias.shape, lambda i: (0, 0, 0)),
                  state],
        out_specs=[fwd(GLA_WIDTH), bwd(GLA_WIDTH), state],
        out_shape=[jax.ShapeDtypeStruct((b, n, GLA_WIDTH), F32),
                   jax.ShapeDtypeStruct((b, n, GLA_WIDTH), F32),
                   jax.ShapeDtypeStruct((b, 2, GLA_WIDTH, GLA_QK), F32)],
        scratch_shapes=[pltpu.VMEM((b, 2, GLA_WIDTH, GLA_QK), F32)],
        compiler_params=pltpu.CompilerParams(dimension_semantics=("arbitrary",),
                                             vmem_limit_bytes=VMEM_LIMIT),
        name="gla_bidir",
    )(gqk, gv, down, gqk, gv, down, gu, bias, s0)


def _block_diag_queries(qt):
    row = lax.broadcasted_iota(jnp.int32, qt.shape, 0)
    zero = jnp.zeros_like(qt)
    return jnp.concatenate([jnp.where(row < DIFF_DH, qt, zero),
                            jnp.where(row >= DIFF_DH, qt, zero)], axis=1)


def _attn_finish(lq_ref, lk_ref, g_ref, o_ref, acc, l, tq, lam_init):
    lqk = jnp.sum(lq_ref[...] * lk_ref[...], axis=1, keepdims=True)
    e = jnp.exp(lqk)
    lam = e[0:1] - e[1:2] + lam_init
    acc = acc * (1.0 / l)
    ot = acc[:, 0:tq] - lam * acc[:, tq:2 * tq]
    ms = jnp.mean(ot * ot, axis=0, keepdims=True)
    y = ot * lax.rsqrt(ms + EPS) * (g_ref[...] * (1.0 - lam_init))
    o_ref[0] = y.T.astype(BF16)


def _attn_online_kernel(lq_ref, lk_ref, g_ref, qt_ref, kc_ref, kl_ref, vtc_ref, vtl_ref, o_ref,
                        m_ref, l_ref, acc_ref, *, lam_init):
    tq = qt_ref.shape[2]
    q_bd = _block_diag_queries(qt_ref[0])

    s = _dot(kc_ref[0], q_bd)
    m0 = jnp.max(s, axis=0, keepdims=True)
    p = jnp.exp2(s - m0)
    m_ref[...] = m0
    l_ref[...] = jnp.sum(p, axis=0, keepdims=True)
    acc_ref[...] = _dot(vtc_ref[0], p.astype(BF16))

    tk = vtl_ref.shape[3]

    def body(j, carry):
        s = _dot(kl_ref[0, pl.ds(pl.multiple_of(j * tk, tk), tk), :], q_bd)
        m_old = m_ref[...]
        m_new = jnp.maximum(m_old, jnp.max(s, axis=0, keepdims=True))
        alpha = jnp.exp2(m_old - m_new)
        p = jnp.exp2(s - m_new)
        m_ref[...] = m_new
        l_ref[...] = alpha * l_ref[...] + jnp.sum(p, axis=0, keepdims=True)
        acc_ref[...] = alpha * acc_ref[...] + _dot(vtl_ref[0, j], p.astype(BF16))
        return carry

    lax.fori_loop(0, vtl_ref.shape[1], body, 0)
    _attn_finish(lq_ref, lk_ref, g_ref, o_ref, acc_ref[...], l_ref[...], tq, lam_init)


def _attn_bounded_kernel(lq_ref, lk_ref, g_ref, qt_ref, kc_ref, kl_ref, vtc_ref, vtl_ref, o_ref,
                         s_ref, *, lam_init):
    tq = qt_ref.shape[2]
    tk = vtl_ref.shape[3]
    q_bd = _block_diag_queries(qt_ref[0])
    n_ctx = kc_ref.shape[1]
    tiles = [(n_ctx, lambda: kc_ref[0], lambda: vtc_ref[0])]
    for j in range(vtl_ref.shape[1]):
        tiles.append((tk, lambda j=j: kl_ref[0, j * tk:(j + 1) * tk, :], lambda j=j: vtl_ref[0, j]))

    s_ref[0, 0:n_ctx, :] = _dot(tiles[0][1](), q_bd)
    l = jnp.zeros((1, 2 * tq), F32)
    acc = jnp.zeros((DIFF_DV, 2 * tq), F32)
    for t, (rows, _, vt) in enumerate(tiles):
        if t + 1 < len(tiles):
            nxt = tiles[t + 1]
            s_ref[(t + 1) % 2, 0:nxt[0], :] = _dot(nxt[1](), q_bd)
        p = jnp.exp2(s_ref[t % 2, 0:rows, :])
        l = l + jnp.sum(p, axis=0, keepdims=True)
        acc = acc + _dot(vt(), p.astype(BF16))
    _attn_finish(lq_ref, lk_ref, g_ref, o_ref, acc, l, tq, lam_init)


def _attn_call(body, lq, lk, g_col, qt, k_ctx, k_lat, vt_ctx, vt_lat, *, lam_init, tq):
    b, _, n = qt.shape
    n_ctx = k_ctx.shape[1]
    nch, tk = vt_lat.shape[1], vt_lat.shape[3]
    const = lambda shape: pl.BlockSpec(shape, lambda bi, h, qi: tuple(0 for _ in shape))
    return pl.pallas_call(
        functools.partial(body, lam_init=lam_init),
        grid=(b, DIFF_HEADS, n // tq),
        in_specs=[const(lq.shape), const(lk.shape), const(g_col.shape),
                  pl.BlockSpec((1, DIFF_DV, tq), lambda bi, h, qi: (bi, h, qi)),
                  pl.BlockSpec((1, n_ctx, DIFF_DV), lambda bi, h, qi: (bi, 0, h)),
                  pl.BlockSpec((1, n, DIFF_DV), lambda bi, h, qi: (bi, 0, h)),
                  pl.BlockSpec((1, DIFF_DV, n_ctx), lambda bi, h, qi: (bi, h, 0)),
                  pl.BlockSpec((1, nch, DIFF_DV, tk), lambda bi, h, qi: (bi, 0, h, 0))],
        out_specs=pl.BlockSpec((1, tq, DIFF_DV), lambda bi, h, qi: (bi, qi, h)),
        out_shape=jax.ShapeDtypeStruct((b, n, DIFF_WIDTH), BF16),
        scratch_shapes=(
            [pltpu.VMEM((2, max(tk, n_ctx), 2 * tq), F32)] if body is _attn_bounded_kernel else
            [pltpu.VMEM((1, 2 * tq), F32), pltpu.VMEM((1, 2 * tq), F32), pltpu.VMEM((DIFF_DV, 2 * tq), F32)]),
        compiler_params=pltpu.CompilerParams(
            dimension_semantics=("arbitrary", "arbitrary", "arbitrary"),
            vmem_limit_bytes=VMEM_LIMIT),
        name=body.__name__.strip("_"),
    )(lq, lk, g_col, qt, k_ctx, k_lat, vt_ctx, vt_lat)


def _out_ffn_kernel(x_ref, of_ref, ob_ref, r_ref, od_ref, gt1_ref, sh2_ref, sc2_ref, gt2_ref,
                    gg_ref, g2_ref, wo_ref, wg_ref, wu_ref, wd_ref, o_ref, acc_ref):
    og = of_ref[0] + ob_ref[0]
    r = r_ref[0].astype(F32)
    parts = []
    for hd in range(GLA_HEADS):
        sl = slice(hd * GLA_DV, (hd + 1) * GLA_DV)
        oh = og[:, sl]
        ms = jnp.mean(oh * oh, axis=-1, keepdims=True)
        parts.append((oh * lax.rsqrt(ms + EPS) * gg_ref[...] * _silu(r[:, sl])).astype(BF16))
    gla = jnp.concatenate(parts, axis=1)
    mix = _dot(gla, wo_ref[0:GLA_WIDTH, :]) + _dot(od_ref[0], wo_ref[GLA_WIDTH:, :])
    x1 = x_ref[0] + gt1_ref[0] * mix
    ms = jnp.mean(x1 * x1, axis=-1, keepdims=True)
    h = x1 * lax.rsqrt(ms + EPS) * g2_ref[...]
    hb = (h * (1.0 + sc2_ref[0]) + sh2_ref[0]).astype(BF16)

    acc_ref[...] = jnp.zeros_like(acc_ref)

    def body(j, carry):
        gate = _dot(hb, wg_ref[j])
        up = _dot(hb, wu_ref[j])
        acc_ref[...] += _dot((_silu(gate) * up).astype(BF16), wd_ref[j])
        return carry

    lax.fori_loop(0, wg_ref.shape[0], body, 0)
    o_ref[0] = x1 + gt2_ref[0] * acc_ref[...]


def _out_ffn_call(x, o_f, o_b, gr, od, gt1, sh2, sc2, gt2, gg, g2, wo, wg, wu, wd, *, tm):
    b, n, d = x.shape
    tok = lambda width: pl.BlockSpec((1, tm, width), lambda i, t: (i, t, 0))
    row = pl.BlockSpec((1, 1, d), lambda i, t: (i, 0, 0))
    const = lambda shape: pl.BlockSpec(shape, lambda i, t: tuple(0 for _ in shape),
                                       pipeline_mode=pl.Buffered(1))
    return pl.pallas_call(
        _out_ffn_kernel,
        grid=(b, n // tm),
        in_specs=[tok(d), tok(GLA_WIDTH), tok(GLA_WIDTH), tok(GLA_WIDTH), tok(DIFF_WIDTH),
                  row, row, row, row, const(gg.shape), const(g2.shape),
                  const(wo.shape), const(wg.shape), const(wu.shape), const(wd.shape)],
        out_specs=tok(d),
        out_shape=jax.ShapeDtypeStruct((b, n, d), F32),
        scratch_shapes=[pltpu.VMEM((tm, d), F32)],
        compiler_params=pltpu.CompilerParams(dimension_semantics=("arbitrary", "arbitrary"),
                                             vmem_limit_bytes=VMEM_LIMIT),
        name="out_proj_ffn",
    )(x, o_f, o_b, gr, od, gt1, sh2, sc2, gt2, gg, g2, wo, wg, wu, wd)


def _rope_tables(n):
    t = jnp.arange(n)
    pos = jnp.stack([(t // GRID_W).astype(F32), (t % GRID_W).astype(F32)], axis=1)
    inv_freq = ROPE_BASE ** (-jnp.arange(ROPE_HALF, dtype=F32) / ROPE_HALF)
    ang = pos[:, :, None] * inv_freq
    cos = jnp.cos(ang)[:, :, None, :]
    sin = jnp.sin(ang)[:, :, None, :] * jnp.array([-1.0, 1.0], F32)[None, None, :, None]
    cos = jnp.broadcast_to(cos, (n, 2, 2, ROPE_HALF)).reshape(n, DIFF_DH)
    sin = jnp.broadcast_to(sin, (n, 2, 2, ROPE_HALF)).reshape(n, DIFF_DH)
    return jnp.tile(cos, (1, LANES // DIFF_DH)), jnp.tile(sin, (1, LANES // DIFF_DH))


def _proj_weight(w_in):
    sizes = (GLA_QK, GLA_QK, GLA_WIDTH, GLA_WIDTH, 2 * GLA_GATE_RANK, DIFF_WIDTH, DIFF_WIDTH, DIFF_WIDTH)
    offs = [0]
    for s in sizes:
        offs.append(offs[-1] + s)
    part = lambda i: w_in[:, offs[i]:offs[i + 1]]
    pad = jnp.zeros((w_in.shape[0], DOWN_PAD - 2 * GLA_GATE_RANK), w_in.dtype)
    return jnp.concatenate([part(0), part(1), part(2), part(3), part(5), part(6), part(7), part(4), pad],
                           axis=1).astype(BF16)


def _tile_rows(n, pref):
    for t in pref:
        if n % t == 0:
            return t
    raise ValueError(f"no row tile for {n}")


def kernel(x, c, ctx, c_ctx, w_mod, b_mod, norm1_g, w_in, gla_gate_up, gla_gate_bias, gla_norm_g,
           diff_q_norm_g, diff_k_norm_g, diff_lambda_q, diff_lambda_k, diff_norm_g, w_out, norm2_g,
           w_ffn_in, w_ffn_out):
    assert w_mod.shape[0] == 1, "single-layer stack"
    b, n, d = x.shape
    n_ctx = ctx.shape[1]
    lam_init = 0.8 - 0.6 * math.exp(-0.3 * 0)

    cond = jnp.concatenate([c, c_ctx[None, :], jnp.zeros((-(b + 1) % 8, d), F32)], axis=0)
    mod = _mod_call(cond, w_mod[0], b_mod[0][None, :])
    sh1, sc1, gt1, sh2, sc2, gt2 = [mod[:, i * d:(i + 1) * d] for i in range(6)]
    lat = lambda m: m[0:b, None, :]
    ctx_rows = lambda m: jnp.broadcast_to(m[b:b + 1, None, :], (b, 1, d))

    w_p = _proj_weight(w_in[0])
    g1 = norm1_g[0][None, :]
    qg = jnp.tile(diff_q_norm_g[0], DIFF_WIDTH // DIFF_DH)[None, :]
    kg = jnp.tile(diff_k_norm_g[0], DIFF_WIDTH // DIFF_DH)[None, :]
    gid = jnp.arange(DIFF_WIDTH) // DIFF_DH
    bd = (gid[:, None] == gid[None, :]).astype(BF16)
    cos_t, sin_t = _rope_tables(n)

    tm_c = _tile_rows(n_ctx, (256, 128, 64))
    tm_x = _tile_rows(n, (512, 256, 128, 64))
    pc = _proj_call(ctx, ctx_rows(sh1), ctx_rows(sc1), g1, w_p, qg, kg, cos_t[:n_ctx], sin_t[:n_ctx], bd,
                    rope=False, tm=tm_c)
    px = _proj_call(x, lat(sh1), lat(sc1), g1, w_p, qg, kg, cos_t, sin_t, bd, rope=True, tm=tm_x)
    gqk_c, gv_c, _, down_c, _, dk_c, vt_c = pc
    gqk, gv, gr, down, qt, dk, vt_lat = px

    gu = jnp.zeros((2, DOWN_PAD, GLA_QK), F32)
    for z in range(2):
        gu = gu.at[z, z * GLA_GATE_RANK:(z + 1) * GLA_GATE_RANK, :].set(gla_gate_up[0, z])
    bias = gla_gate_bias[0][:, None, :]
    zero_state = jnp.zeros((b, 2, GLA_WIDTH, GLA_QK), F32)
    _, _, s_ctx = _gla_call(gqk_c, gv_c, down_c, gu, bias, zero_state)
    o_f, o_b, _ = _gla_call(gqk, gv, down, gu, bias, s_ctx)

    tq = _tile_rows(n, (256, 128))
    assert n_ctx == tm_c, "context keys are one attention tile"
    vt_ctx = vt_c[:, 0]
    score_bound = 8.0 * jnp.max(jnp.abs(diff_q_norm_g[0])) * jnp.max(jnp.abs(diff_k_norm_g[0]))
    attn_args = (diff_lambda_q[0], diff_lambda_k[0], diff_norm_g[0][:, None], qt, dk_c, dk, vt_ctx, vt_lat)
    od = lax.cond(
        score_bound <= SCORE_BOUND,
        lambda *a: _attn_call(_attn_bounded_kernel, *a, lam_init=lam_init, tq=tq),
        lambda *a: _attn_call(_attn_online_kernel, *a, lam_init=lam_init, tq=tq),
        *attn_args)

    fh = w_ffn_out.shape[1]
    fc = 256
    wg = w_ffn_in[0][:, :fh].reshape(d, fh // fc, fc).transpose(1, 0, 2).astype(BF16)
    wu = w_ffn_in[0][:, fh:].reshape(d, fh // fc, fc).transpose(1, 0, 2).astype(BF16)
    wd = w_ffn_out[0].reshape(fh // fc, fc, d).astype(BF16)
    return _out_ffn_call(x, o_f, o_b, gr, od, lat(gt1), lat(sh2), lat(sc2), lat(gt2),
                         gla_norm_g[0][None, :], norm2_g[0][None, :], w_out[0].astype(BF16), wg, wu, wd,
                         tm=tm_x)
```

```python
import functools
import math

import jax
import jax.numpy as jnp
from jax import lax
from jax.experimental import pallas as pl
from jax.experimental.pallas import tpu as pltpu

F32 = jnp.float32
BF16 = jnp.bfloat16
HIGHEST = lax.Precision.HIGHEST

EPS = 1e-6
GRID_W = 64
GLA_HEADS = 4
GLA_DK = 64
GLA_DV = 128
GLA_QK = GLA_HEADS * GLA_DK
GLA_WIDTH = GLA_HEADS * GLA_DV
GLA_GATE_RANK = 16
GLA_GATE_NORM = 16.0
GLA_CHUNK = 64
DIFF_HEADS = 4
DIFF_DH = 64
DIFF_DV = 128
DIFF_WIDTH = DIFF_HEADS * DIFF_DV
ROPE_BASE = 10000.0
ROPE_AXIS_DIM = DIFF_DH // 2
ROPE_HALF = ROPE_AXIS_DIM // 2
LANES = 128
DOWN_PAD = LANES
LOG2_E = math.log2(math.e)
SCORE_BOUND = 75.0

VMEM_LIMIT = 56 * 1024 * 1024


def _dot(a, b, precision=None):
    return jnp.dot(a, b, preferred_element_type=F32, precision=precision)


def _dot_nt(a, b):
    return lax.dot_general(a, b, (((1,), (1,)), ((), ())), preferred_element_type=F32)


def _dot_tn(a, b):
    return lax.dot_general(a, b, (((0,), (0,)), ((), ())), preferred_element_type=F32)


def _silu(x):
    return x / (1.0 + jnp.exp(-x))


def _mod_kernel(c_ref, w_ref, b_ref, o_ref):
    o_ref[...] = _dot(_silu(c_ref[...]), w_ref[...], HIGHEST) + b_ref[...]


def _mod_call(cond, w_mod, b_mod):
    rows, d = cond.shape
    cols = w_mod.shape[1]
    bn = cols // 4
    return pl.pallas_call(
        _mod_kernel,
        grid=(cols // bn,),
        in_specs=[pl.BlockSpec((rows, d), lambda j: (0, 0)),
                  pl.BlockSpec((d, bn), lambda j: (0, j)),
                  pl.BlockSpec((1, bn), lambda j: (0, j))],
        out_specs=pl.BlockSpec((rows, bn), lambda j: (0, j)),
        out_shape=jax.ShapeDtypeStruct((rows, cols), F32),
        compiler_params=pltpu.CompilerParams(dimension_semantics=("arbitrary",),
                                             vmem_limit_bytes=VMEM_LIMIT),
        name="adaln_mod",
    )(cond, w_mod, b_mod)


_C_GQ, _C_GK, _C_GV, _C_GR, _C_DQ, _C_DK, _C_DV, _C_DOWN, _C_END = (
    0, 256, 512, 1024, 1536, 2048, 2560, 3072, 3072 + DOWN_PAD)


def _swap_halves(y):
    lane = lax.broadcasted_iota(jnp.int32, y.shape, 1)
    upper = (lane & ROPE_HALF) != 0
    return jnp.where(upper, pltpu.roll(y, ROPE_HALF, 1), pltpu.roll(y, LANES - ROPE_HALF, 1))


def _proj_kernel(x_ref, sh_ref, sc_ref, g1_ref, w_ref, qg_ref, kg_ref, cos_ref, sin_ref, bd_ref,
                 gqk_o, gv_o, gr_o, down_o, dq_o, dk_o, dv_o, *, rope):
    x = x_ref[0]
    ms = jnp.mean(x * x, axis=-1, keepdims=True)
    h = x * lax.rsqrt(ms + EPS) * g1_ref[...]
    h = h * (1.0 + sc_ref[0]) + sh_ref[0]
    hb = h.astype(BF16)

    def mm(lo, hi):
        return _dot(hb, w_ref[:, lo:hi])

    gqk_o[0, :, 0:GLA_QK] = (mm(_C_GQ, _C_GK) * (GLA_DK ** -0.5)).astype(BF16)
    gqk_o[0, :, GLA_QK:2 * GLA_QK] = mm(_C_GK, _C_GV).astype(BF16)
    gv_o[0] = mm(_C_GV, _C_GR).astype(BF16)
    gr_o[0] = mm(_C_GR, _C_DQ).astype(BF16)
    dv = mm(_C_DV, _C_DOWN)
    for s in range(DIFF_WIDTH // LANES):
        dv_o[0, 0, s * LANES:(s + 1) * LANES, :] = dv[:, s * LANES:(s + 1) * LANES].T.astype(BF16)
    down_o[0] = mm(_C_DOWN, _C_END)

    def qk_norm(y, g_ref, scale, store):
        ss = _dot((y * y).astype(BF16), bd_ref[...])
        yn = y * lax.rsqrt(ss * (1.0 / DIFF_DH) + EPS) * g_ref[...]
        for s in range(DIFF_WIDTH // LANES):
            slab = yn[:, s * LANES:(s + 1) * LANES]
            if rope:
                slab = slab * cos_ref[...] + _swap_halves(slab) * sin_ref[...]
            store(s, slab * scale)

    def store_q(s, slab):
        dq_o[0, s * LANES:(s + 1) * LANES, :] = slab.T.astype(BF16)

    def store_k(s, slab):
        dk_o[0, :, s * LANES:(s + 1) * LANES] = slab.astype(BF16)

    qk_norm(mm(_C_DQ, _C_DK), qg_ref, DIFF_DH ** -0.5 * LOG2_E, store_q)
    qk_norm(mm(_C_DK, _C_DV), kg_ref, 1.0, store_k)


def _proj_call(x, sh, sc, g1, w, qg, kg, cos_t, sin_t, bd, *, rope, tm):
    b, n, d = x.shape
    assert n % tm == 0
    tok = lambda width: pl.BlockSpec((1, tm, width), lambda i, t: (i, t, 0))
    const = lambda shape: pl.BlockSpec(shape, lambda i, t: tuple(0 for _ in shape))
    row = pl.BlockSpec((1, 1, d), lambda i, t: (i, 0, 0))
    outs = [(2 * GLA_QK, BF16), (GLA_WIDTH, BF16), (GLA_WIDTH, BF16), (DOWN_PAD, F32), (DIFF_WIDTH, BF16)]
    out_specs = [tok(wd) for wd, _ in outs]
    out_shape = [jax.ShapeDtypeStruct((b, n, wd), dt) for wd, dt in outs]
    out_specs.insert(4, pl.BlockSpec((1, DIFF_WIDTH, tm), lambda i, t: (i, 0, t)))
    out_shape.insert(4, jax.ShapeDtypeStruct((b, DIFF_WIDTH, n), BF16))
    out_specs.append(pl.BlockSpec((1, 1, DIFF_WIDTH, tm), lambda i, t: (i, t, 0, 0)))
    out_shape.append(jax.ShapeDtypeStruct((b, n // tm, DIFF_WIDTH, tm), BF16))
    return pl.pallas_call(
        functools.partial(_proj_kernel, rope=rope),
        grid=(b, n // tm),
        in_specs=[tok(d), row, row, const((1, d)), const(w.shape), const((1, DIFF_WIDTH)),
                  const((1, DIFF_WIDTH)),
                  pl.BlockSpec((tm, LANES), lambda i, t: (t, 0)),
                  pl.BlockSpec((tm, LANES), lambda i, t: (t, 0)),
                  const(bd.shape)],
        out_specs=out_specs,
        out_shape=out_shape,
        compiler_params=pltpu.CompilerParams(dimension_semantics=("arbitrary", "arbitrary"),
                                             vmem_limit_bytes=VMEM_LIMIT),
        name="in_proj_rope" if rope else "in_proj_ctx",
    )(x, sh, sc, g1, w, qg, kg, cos_t, sin_t, bd)


def _split_bf16(x):
    hi = x.astype(BF16)
    return hi, (x - hi.astype(F32)).astype(BF16)


def _gla_prepare(qk_ref, dn_ref, gu, bias, tri, upper):
    nb, c = qk_ref.shape[0], qk_ref.shape[1]
    rows = nb * c
    down = dn_ref[...].reshape(rows, DOWN_PAD)
    dh, dl = _split_bf16(down)
    gh, gl = gu
    logits = _dot(dh, gh) + (_dot(dl, gh) + _dot(dh, gl)) + bias
    la = (jnp.minimum(logits, 0.0) - jnp.log(1.0 + jnp.exp(-jnp.abs(logits)))) * (1.0 / GLA_GATE_NORM)
    lh, ll = _split_bf16(la)
    cum_all = _dot(tri, lh) + _dot(tri, ll)

    q_in, q_mid, k_mid, k_end, tots = [], [], [], [], []
    for bi in range(nb):
        cum = cum_all[bi * c:(bi + 1) * c]
        tot = cum[0:1] if upper else cum[c - 1:c]
        mid = cum[c // 2:c // 2 + 1]
        q = qk_ref[bi, :, 0:GLA_QK].astype(F32)
        k = qk_ref[bi, :, GLA_QK:2 * GLA_QK].astype(F32)
        q_in.append((q * jnp.exp(cum)).astype(BF16))
        q_mid.append(q * jnp.exp(cum - mid))
        k_mid.append((k * jnp.exp(mid - cum)).astype(BF16))
        k_end.append((k * jnp.exp(tot - cum)).astype(BF16))
        tots.append(tot)
    return q_in, jnp.concatenate(q_mid, axis=0), jnp.concatenate(k_mid, axis=0), k_end, tots


def _gla_apply(prepared, v_ref, st_ref, z, o_ref, head_lane, state_mask, tri_mask):
    q_in, q_mid, k_mid, k_end, tots = prepared
    nb, c = v_ref.shape[0], v_ref.shape[1]
    v_all = v_ref[...].reshape(nb * c, GLA_WIDTH)
    outs = []
    for hd in range(GLA_HEADS):
        qh = jnp.where(head_lane == hd, q_mid, 0.0).astype(BF16)
        a = jnp.where(tri_mask, _dot_nt(qh, k_mid), 0.0).astype(BF16)
        outs.append(_dot(a, v_all[:, hd * GLA_DV:(hd + 1) * GLA_DV]))
    o_intra = jnp.concatenate(outs, axis=1)

    for bi in range(nb):
        st = st_ref[bi, z]
        o_ref[bi] = o_intra[bi * c:(bi + 1) * c] + _dot_nt(q_in[bi], st.astype(BF16))
        ut = _dot_tn(v_ref[bi], k_end[bi])
        st_ref[bi, z] = jnp.exp(tots[bi]) * st + jnp.where(state_mask, ut, 0.0)


def _gla_kernel(qk_f, v_f, dn_f, qk_b, v_b, dn_b, gu_ref, bias_ref, s0_ref,
                of_ref, ob_ref, sout_ref, st_ref):
    i = pl.program_id(0)
    nb, c = qk_f.shape[0], qk_f.shape[1]
    rows = nb * c

    @pl.when(i == 0)
    def _():
        st_ref[...] = s0_ref[...]

    r = lax.broadcasted_iota(jnp.int32, (rows, rows), 0)
    cc = lax.broadcasted_iota(jnp.int32, (rows, rows), 1)
    same = (r // c) == (cc // c)
    lower = same & (r >= cc)
    upper = same & (r <= cc)
    head_lane = lax.broadcasted_iota(jnp.int32, (rows, GLA_QK), 1) // GLA_DK
    sr = lax.broadcasted_iota(jnp.int32, (GLA_WIDTH, GLA_QK), 0) // GLA_DV
    sl = lax.broadcasted_iota(jnp.int32, (GLA_WIDTH, GLA_QK), 1) // GLA_DK
    state_mask = sr == sl

    prep_f = _gla_prepare(qk_f, dn_f, _split_bf16(gu_ref[0]), bias_ref[0], lower.astype(BF16), False)
    prep_b = _gla_prepare(qk_b, dn_b, _split_bf16(gu_ref[1]), bias_ref[1], upper.astype(BF16), True)
    _gla_apply(prep_f, v_f, st_ref, 0, of_ref, head_lane, state_mask, lower)
    _gla_apply(prep_b, v_b, st_ref, 1, ob_ref, head_lane, state_mask, upper)

    @pl.when(i == pl.num_programs(0) - 1)
    def _():
        sout_ref[...] = st_ref[...]


def _gla_call(gqk, gv, down, gu, bias, s0):
    b, n, _ = gqk.shape
    c = GLA_CHUNK
    nc = n // c
    fwd = lambda width: pl.BlockSpec((b, c, width), lambda i: (0, i, 0))
    bwd = lambda width: pl.BlockSpec((b, c, width), lambda i: (0, nc - 1 - i, 0))
    state = pl.BlockSpec((b, 2, GLA_WIDTH, GLA_QK), lambda i: (0, 0, 0, 0))
    return pl.pallas_call(
        _gla_kernel,
        grid=(nc,),
        in_specs=[fwd(2 * GLA_QK), fwd(GLA_WIDTH), fwd(DOWN_PAD),
                  bwd(2 * GLA_QK), bwd(GLA_WIDTH), bwd(DOWN_PAD),
                  pl.BlockSpec(gu.shape, lambda i: (0, 0, 0)),
                  pl.BlockSpec(bias.shape, lambda i: (0, 0, 0)),
                  state],
        out_specs=[fwd(GLA_WIDTH), bwd(GLA_WIDTH), state],
        out_shape=[jax.ShapeDtypeStruct((b, n, GLA_WIDTH), F32),
                   jax.ShapeDtypeStruct((b, n, GLA_WIDTH), F32),
                   jax.ShapeDtypeStruct((b, 2, GLA_WIDTH, GLA_QK), F32)],
        scratch_shapes=[pltpu.VMEM((b, 2, GLA_WIDTH, GLA_QK), F32)],
        compiler_params=pltpu.CompilerParams(dimension_semantics=("arbitrary",),
                                             vmem_limit_bytes=VMEM_LIMIT),
        name="gla_bidir",
    )(gqk, gv, down, gqk, gv, down, gu, bias, s0)


def _block_diag_queries(qt):
    row = lax.broadcasted_iota(jnp.int32, qt.shape, 0)
    zero = jnp.zeros_like(qt)
    return jnp.concatenate([jnp.where(row < DIFF_DH, qt, zero),
                            jnp.where(row >= DIFF_DH, qt, zero)], axis=1)


def _attn_finish(lq_ref, lk_ref, g_ref, o_ref, acc, l, tq, lam_init):
    lqk = jnp.sum(lq_ref[...] * lk_ref[...], axis=1, keepdims=True)
    e = jnp.exp(lqk)
    lam = e[0:1] - e[1:2] + lam_init
    acc = acc * (1.0 / l)
    ot = acc[:, 0:tq] - lam * acc[:, tq:2 * tq]
    ms = jnp.mean(ot * ot, axis=0, keepdims=True)
    y = ot * lax.rsqrt(ms + EPS) * (g_ref[...] * (1.0 - lam_init))
    o_ref[0] = y.T.astype(BF16)


def _attn_online_kernel(lq_ref, lk_ref, g_ref, qt_ref, kc_ref, kl_ref, vtc_ref, vtl_ref, o_ref,
                        m_ref, l_ref, acc_ref, *, lam_init):
    tq = qt_ref.shape[2]
    q_bd = _block_diag_queries(qt_ref[0])

    s = _dot(kc_ref[0], q_bd)
    m0 = jnp.max(s, axis=0, keepdims=True)
    p = jnp.exp2(s - m0)
    m_ref[...] = m0
    l_ref[...] = jnp.sum(p, axis=0, keepdims=True)
    acc_ref[...] = _dot(vtc_ref[0], p.astype(BF16))

    tk = vtl_ref.shape[3]

    def body(j, carry):
        s = _dot(kl_ref[0, pl.ds(pl.multiple_of(j * tk, tk), tk), :], q_bd)
        m_old = m_ref[...]
        m_new = jnp.maximum(m_old, jnp.max(s, axis=0, keepdims=True))
        alpha = jnp.exp2(m_old - m_new)
        p = jnp.exp2(s - m_new)
        m_ref[...] = m_new
        l_ref[...] = alpha * l_ref[...] + jnp.sum(p, axis=0, keepdims=True)
        acc_ref[...] = alpha * acc_ref[...] + _dot(vtl_ref[0, j], p.astype(BF16))
        return carry

    lax.fori_loop(0, vtl_ref.shape[1], body, 0)
    _attn_finish(lq_ref, lk_ref, g_ref, o_ref, acc_ref[...], l_ref[...], tq, lam_init)


def _attn_bounded_kernel(lq_ref, lk_ref, g_ref, qt_ref, kc_ref, kl_ref, vtc_ref, vtl_ref, o_ref,
                         s_ref, *, lam_init):
    tq = qt_ref.shape[2]
    tk = vtl_ref.shape[3]
    q_bd = _block_diag_queries(qt_ref[0])
    n_ctx = kc_ref.shape[1]
    tiles = [(n_ctx, lambda: kc_ref[0], lambda: vtc_ref[0])]
    for j in range(vtl_ref.shape[1]):
        tiles.append((tk, lambda j=j: kl_ref[0, j * tk:(j + 1) * tk, :], lambda j=j: vtl_ref[0, j]))

    s_ref[0, 0:n_ctx, :] = _dot(tiles[0][1](), q_bd)
    l = jnp.zeros((1, 2 * tq), F32)
    acc = jnp.zeros((DIFF_DV, 2 * tq), F32)
    for t, (rows, _, vt) in enumerate(tiles):
        if t + 1 < len(tiles):
            nxt = tiles[t + 1]
            s_ref[(t + 1) % 2, 0:nxt[0], :] = _dot(nxt[1](), q_bd)
        p = jnp.exp2(s_ref[t % 2, 0:rows, :])
        l = l + jnp.sum(p, axis=0, keepdims=True)
        acc = acc + _dot(vt(), p.astype(BF16))
    _attn_finish(lq_ref, lk_ref, g_ref, o_ref, acc, l, tq, lam_init)


def _attn_call(body, lq, lk, g_col, qt, k_ctx, k_lat, vt_ctx, vt_lat, *, lam_init, tq):
    b, _, n = qt.shape
    n_ctx = k_ctx.shape[1]
    nch, tk = vt_lat.shape[1], vt_lat.shape[3]
    const = lambda shape: pl.BlockSpec(shape, lambda bi, h, qi: tuple(0 for _ in shape))
    return pl.pallas_call(
        functools.partial(body, lam_init=lam_init),
        grid=(b, DIFF_HEADS, n // tq),
        in_specs=[const(lq.shape), const(lk.shape), const(g_col.shape),
                  pl.BlockSpec((1, DIFF_DV, tq), lambda bi, h, qi: (bi, h, qi)),
                  pl.BlockSpec((1, n_ctx, DIFF_DV), lambda bi, h, qi: (bi, 0, h)),
                  pl.BlockSpec((1, n, DIFF_DV), lambda bi, h, qi: (bi, 0, h)),
                  pl.BlockSpec((1, DIFF_DV, n_ctx), lambda bi, h, qi: (bi, h, 0)),
                  pl.BlockSpec((1, nch, DIFF_DV, tk), lambda bi, h, qi: (bi, 0, h, 0))],
        out_specs=pl.BlockSpec((1, tq, DIFF_DV), lambda bi, h, qi: (bi, qi, h)),
        out_shape=jax.ShapeDtypeStruct((b, n, DIFF_WIDTH), BF16),
        scratch_shapes=(
            [pltpu.VMEM((2, max(tk, n_ctx), 2 * tq), F32)] if body is _attn_bounded_kernel else
            [pltpu.VMEM((1, 2 * tq), F32), pltpu.VMEM((1, 2 * tq), F32), pltpu.VMEM((DIFF_DV, 2 * tq), F32)]),
        compiler_params=pltpu.CompilerParams(
            dimension_semantics=("arbitrary", "arbitrary", "arbitrary"),
            vmem_limit_bytes=VMEM_LIMIT),
        name=body.__name__.strip("_"),
    )(lq, lk, g_col, qt, k_ctx, k_lat, vt_ctx, vt_lat)


def _out_ffn_kernel(x_ref, of_ref, ob_ref, r_ref, od_ref, gt1_ref, sh2_ref, sc2_ref, gt2_ref,
                    gg_ref, g2_ref, wo_ref, wg_ref, wu_ref, wd_ref, o_ref, acc_ref):
    og = of_ref[0] + ob_ref[0]
    r = r_ref[0].astype(F32)
    parts = []
    for hd in range(GLA_HEADS):
        sl = slice(hd * GLA_DV, (hd + 1) * GLA_DV)
        oh = og[:, sl]
        ms = jnp.mean(oh * oh, axis=-1, keepdims=True)
        parts.append((oh * lax.rsqrt(ms + EPS) * gg_ref[...] * _silu(r[:, sl])).astype(BF16))
    gla = jnp.concatenate(parts, axis=1)
    mix = _dot(gla, wo_ref[0:GLA_WIDTH, :]) + _dot(od_ref[0], wo_ref[GLA_WIDTH:, :])
    x1 = x_ref[0] + gt1_ref[0] * mix
    ms = jnp.mean(x1 * x1, axis=-1, keepdims=True)
    h = x1 * lax.rsqrt(ms + EPS) * g2_ref[...]
    hb = (h * (1.0 + sc2_ref[0]) + sh2_ref[0]).astype(BF16)

    acc_ref[...] = jnp.zeros_like(acc_ref)

    def body(j, carry):
        gate = _dot(hb, wg_ref[j])
        up = _dot(hb, wu_ref[j])
        acc_ref[...] += _dot((_silu(gate) * up).astype(BF16), wd_ref[j])
        return carry

    lax.fori_loop(0, wg_ref.shape[0], body, 0, unroll=True)
    o_ref[0] = x1 + gt2_ref[0] * acc_ref[...]


def _out_ffn_call(x, o_f, o_b, gr, od, gt1, sh2, sc2, gt2, gg, g2, wo, wg, wu, wd, *, tm):
    b, n, d = x.shape
    tok = lambda width: pl.BlockSpec((1, tm, width), lambda i, t: (i, t, 0))
    row = pl.BlockSpec((1, 1, d), lambda i, t: (i, 0, 0))
    const = lambda shape: pl.BlockSpec(shape, lambda i, t: tuple(0 for _ in shape),
                                       pipeline_mode=pl.Buffered(1))
    return pl.pallas_call(
        _out_ffn_kernel,
        grid=(b, n // tm),
        in_specs=[tok(d), tok(GLA_WIDTH), tok(GLA_WIDTH), tok(GLA_WIDTH), tok(DIFF_WIDTH),
                  row, row, row, row, const(gg.shape), const(g2.shape),
                  const(wo.shape), const(wg.shape), const(wu.shape), const(wd.shape)],
        out_specs=tok(d),
        out_shape=jax.ShapeDtypeStruct((b, n, d), F32),
        scratch_shapes=[pltpu.VMEM((tm, d), F32)],
        compiler_params=pltpu.CompilerParams(dimension_semantics=("arbitrary", "arbitrary"),
                                             vmem_limit_bytes=VMEM_LIMIT),
        name="out_proj_ffn",
    )(x, o_f, o_b, gr, od, gt1, sh2, sc2, gt2, gg, g2, wo, wg, wu, wd)


def _rope_tables(n):
    t = jnp.arange(n)
    pos = jnp.stack([(t // GRID_W).astype(F32), (t % GRID_W).astype(F32)], axis=1)
    inv_freq = ROPE_BASE ** (-jnp.arange(ROPE_HALF, dtype=F32) / ROPE_HALF)
    ang = pos[:, :, None] * inv_freq
    cos = jnp.cos(ang)[:, :, None, :]
    sin = jnp.sin(ang)[:, :, None, :] * jnp.array([-1.0, 1.0], F32)[None, None, :, None]
    cos = jnp.broadcast_to(cos, (n, 2, 2, ROPE_HALF)).reshape(n, DIFF_DH)
    sin = jnp.broadcast_to(sin, (n, 2, 2, ROPE_HALF)).reshape(n, DIFF_DH)
    return jnp.tile(cos, (1, LANES // DIFF_DH)), jnp.tile(sin, (1, LANES // DIFF_DH))


def _proj_weight(w_in):
    sizes = (GLA_QK, GLA_QK, GLA_WIDTH, GLA_WIDTH, 2 * GLA_GATE_RANK, DIFF_WIDTH, DIFF_WIDTH, DIFF_WIDTH)
    offs = [0]
    for s in sizes:
        offs.append(offs[-1] + s)
    part = lambda i: w_in[:, offs[i]:offs[i + 1]]
    pad = jnp.zeros((w_in.shape[0], DOWN_PAD - 2 * GLA_GATE_RANK), w_in.dtype)
    return jnp.concatenate([part(0), part(1), part(2), part(3), part(5), part(6), part(7), part(4), pad],
                           axis=1).astype(BF16)


def _tile_rows(n, pref):
    for t in pref:
        if n % t == 0:
            return t
    raise ValueError(f"no row tile for {n}")


def kernel(x, c, ctx, c_ctx, w_mod, b_mod, norm1_g, w_in, gla_gate_up, gla_gate_bias, gla_norm_g,
           diff_q_norm_g, diff_k_norm_g, diff_lambda_q, diff_lambda_k, diff_norm_g, w_out, norm2_g,
           w_ffn_in, w_ffn_out):
    assert w_mod.shape[0] == 1, "single-layer stack"
    b, n, d = x.shape
    n_ctx = ctx.shape[1]
    lam_init = 0.8 - 0.6 * math.exp(-0.3 * 0)

    cond = jnp.concatenate([c, c_ctx[None, :], jnp.zeros((-(b + 1) % 8, d), F32)], axis=0)
    mod = _mod_call(cond, w_mod[0], b_mod[0][None, :])
    sh1, sc1, gt1, sh2, sc2, gt2 = [mod[:, i * d:(i + 1) * d] for i in range(6)]
    lat = lambda m: m[0:b, None, :]
    ctx_rows = lambda m: jnp.broadcast_to(m[b:b + 1, None, :], (b, 1, d))

    w_p = _proj_weight(w_in[0])
    g1 = norm1_g[0][None, :]
    qg = jnp.tile(diff_q_norm_g[0], DIFF_WIDTH // DIFF_DH)[None, :]
    kg = jnp.tile(diff_k_norm_g[0], DIFF_WIDTH // DIFF_DH)[None, :]
    gid = jnp.arange(DIFF_WIDTH) // DIFF_DH
    bd = (gid[:, None] == gid[None, :]).astype(BF16)
    cos_t, sin_t = _rope_tables(n)

    tm_c = _tile_rows(n_ctx, (256, 128, 64))
    tm_x = _tile_rows(n, (512, 256, 128, 64))
    pc = _proj_call(ctx, ctx_rows(sh1), ctx_rows(sc1), g1, w_p, qg, kg, cos_t[:n_ctx], sin_t[:n_ctx], bd,
                    rope=False, tm=tm_c)
    px = _proj_call(x, lat(sh1), lat(sc1), g1, w_p, qg, kg, cos_t, sin_t, bd, rope=True, tm=tm_x)
    gqk_c, gv_c, _, down_c, _, dk_c, vt_c = pc
    gqk, gv, gr, down, qt, dk, vt_lat = px

    gu = jnp.zeros((2, DOWN_PAD, GLA_QK), F32)
    for z in range(2):
        gu = gu.at[z, z * GLA_GATE_RANK:(z + 1) * GLA_GATE_RANK, :].set(gla_gate_up[0, z])
    bias = gla_gate_bias[0][:, None, :]
    zero_state = jnp.zeros((b, 2, GLA_WIDTH, GLA_QK), F32)
    _, _, s_ctx = _gla_call(gqk_c, gv_c, down_c, gu, bias, zero_state)
    o_f, o_b, _ = _gla_call(gqk, gv, down, gu, bias, s_ctx)

    tq = _tile_rows(n, (512, 256, 128))
    assert n_ctx == tm_c, "context keys are one attention tile"
    vt_ctx = vt_c[:, 0]
    score_bound = 8.0 * jnp.max(jnp.abs(diff_q_norm_g[0])) * jnp.max(jnp.abs(diff_k_norm_g[0]))
    attn_args = (diff_lambda_q[0], diff_lambda_k[0], diff_norm_g[0][:, None], qt, dk_c, dk, vt_ctx, vt_lat)
    od = lax.cond(
        score_bound <= SCORE_BOUND,
        lambda *a: _attn_call(_attn_bounded_kernel, *a, lam_init=lam_init, tq=tq),
        lambda *a: _attn_call(_attn_online_kernel, *a, lam_init=lam_init, tq=tq),
        *attn_args)

    fh = w_ffn_out.shape[1]
    fc = 256
    wg = w_ffn_in[0][:, :fh].reshape(d, fh // fc, fc).transpose(1, 0, 2).astype(BF16)
    wu = w_ffn_in[0][:, fh:].reshape(d, fh // fc, fc).transpose(1, 0, 2).astype(BF16)
    wd = w_ffn_out[0].reshape(fh // fc, fc, d).astype(BF16)
    return _out_ffn_call(x, o_f, o_b, gr, od, lat(gt1), lat(sh2), lat(sc2), lat(gt2),
                         gla_norm_g[0][None, :], norm2_g[0][None, :], w_out[0].astype(BF16), wg, wu, wd,
                         tm=tm_x)
```

```python
import functools
import math

import jax
import jax.numpy as jnp
from jax import lax
from jax.experimental import pallas as pl
from jax.experimental.pallas import tpu as pltpu

F32 = jnp.float32
BF16 = jnp.bfloat16
HIGHEST = lax.Precision.HIGHEST

EPS = 1e-6
GRID_W = 64
GLA_HEADS = 4
GLA_DK = 64
GLA_DV = 128
GLA_QK = GLA_HEADS * GLA_DK
GLA_WIDTH = GLA_HEADS * GLA_DV
GLA_GATE_RANK = 16
GLA_GATE_NORM = 16.0
GLA_CHUNK = 64
DIFF_HEADS = 4
DIFF_DH = 64
DIFF_DV = 128
DIFF_WIDTH = DIFF_HEADS * DIFF_DV
ROPE_BASE = 10000.0
ROPE_AXIS_DIM = DIFF_DH // 2
ROPE_HALF = ROPE_AXIS_DIM // 2
LANES = 128
DOWN_PAD = LANES
LOG2_E = math.log2(math.e)
SCORE_BOUND = 75.0

VMEM_LIMIT = 56 * 1024 * 1024
PROJ_SUB_ROWS = 256
FFN_CHUNK = 256


def _dot(a, b, precision=None):
    return jnp.dot(a, b, preferred_element_type=F32, precision=precision)


def _dot_nt(a, b):
    return lax.dot_general(a, b, (((1,), (1,)), ((), ())), preferred_element_type=F32)


def _dot_tn(a, b):
    return lax.dot_general(a, b, (((0,), (0,)), ((), ())), preferred_element_type=F32)


def _silu(x):
    return x / (1.0 + jnp.exp(-x))


def _mod_kernel(c_ref, w_ref, b_ref, o_ref):
    o_ref[...] = _dot(_silu(c_ref[...]), w_ref[...], HIGHEST) + b_ref[...]


def _mod_call(cond, w_mod, b_mod):
    rows, d = cond.shape
    cols = w_mod.shape[1]
    bn = cols // 4
    return pl.pallas_call(
        _mod_kernel,
        grid=(cols // bn,),
        in_specs=[pl.BlockSpec((rows, d), lambda j: (0, 0)),
                  pl.BlockSpec((d, bn), lambda j: (0, j)),
                  pl.BlockSpec((1, bn), lambda j: (0, j))],
        out_specs=pl.BlockSpec((rows, bn), lambda j: (0, j)),
        out_shape=jax.ShapeDtypeStruct((rows, cols), F32),
        compiler_params=pltpu.CompilerParams(dimension_semantics=("arbitrary",),
                                             vmem_limit_bytes=VMEM_LIMIT),
        name="adaln_mod",
    )(cond, w_mod, b_mod)


_C_GQ, _C_GK, _C_GV, _C_GR, _C_DQ, _C_DK, _C_DV, _C_DOWN, _C_END = (
    0, 256, 512, 1024, 1536, 2048, 2560, 3072, 3072 + DOWN_PAD)


def _swap_halves(y):
    lane = lax.broadcasted_iota(jnp.int32, y.shape, 1)
    upper = (lane & ROPE_HALF) != 0
    return jnp.where(upper, pltpu.roll(y, ROPE_HALF, 1), pltpu.roll(y, LANES - ROPE_HALF, 1))


def _proj_kernel(x_ref, sh_ref, sc_ref, g1_ref, w_ref, qg_ref, kg_ref, cos_ref, sin_ref, bd_ref,
                 gqk_o, gv_o, gr_o, down_o, dq_o, dk_o, dv_o, *, rope, sub):
    for r0 in range(0, x_ref.shape[1], sub):
        _proj_rows(slice(r0, r0 + sub), x_ref, sh_ref, sc_ref, g1_ref, w_ref, qg_ref, kg_ref, cos_ref, sin_ref,
                   bd_ref, gqk_o, gv_o, gr_o, down_o, dq_o, dk_o, dv_o, rope)


def _proj_rows(rs, x_ref, sh_ref, sc_ref, g1_ref, w_ref, qg_ref, kg_ref, cos_ref, sin_ref, bd_ref,
               gqk_o, gv_o, gr_o, down_o, dq_o, dk_o, dv_o, rope):
    x = x_ref[0, rs, :]
    ms = jnp.mean(x * x, axis=-1, keepdims=True)
    h = x * lax.rsqrt(ms + EPS) * g1_ref[...]
    h = h * (1.0 + sc_ref[0]) + sh_ref[0]
    hb = h.astype(BF16)

    def mm(lo, hi):
        return _dot(hb, w_ref[:, lo:hi])

    gqk_o[0, rs, 0:GLA_QK] = (mm(_C_GQ, _C_GK) * (GLA_DK ** -0.5)).astype(BF16)
    gqk_o[0, rs, GLA_QK:2 * GLA_QK] = mm(_C_GK, _C_GV).astype(BF16)
    gv_o[0, rs, :] = mm(_C_GV, _C_GR).astype(BF16)
    gr_o[0, rs, :] = mm(_C_GR, _C_DQ).astype(BF16)
    dv = mm(_C_DV, _C_DOWN)
    for s in range(DIFF_WIDTH // LANES):
        dv_o[0, 0, s * LANES:(s + 1) * LANES, rs] = dv[:, s * LANES:(s + 1) * LANES].T.astype(BF16)
    down_o[0, rs, :] = mm(_C_DOWN, _C_END)

    def qk_norm(y, g_ref, scale, store):
        ss = _dot((y * y).astype(BF16), bd_ref[...])
        yn = y * lax.rsqrt(ss * (1.0 / DIFF_DH) + EPS) * g_ref[...]
        for s in range(DIFF_WIDTH // LANES):
            slab = yn[:, s * LANES:(s + 1) * LANES]
            if rope:
                slab = slab * cos_ref[rs, :] + _swap_halves(slab) * sin_ref[rs, :]
            store(s, slab * scale)

    def store_q(s, slab):
        dq_o[0, s * LANES:(s + 1) * LANES, rs] = slab.T.astype(BF16)

    def store_k(s, slab):
        dk_o[0, rs, s * LANES:(s + 1) * LANES] = slab.astype(BF16)

    qk_norm(mm(_C_DQ, _C_DK), qg_ref, DIFF_DH ** -0.5 * LOG2_E, store_q)
    qk_norm(mm(_C_DK, _C_DV), kg_ref, 1.0, store_k)


def _proj_call(x, sh, sc, g1, w, qg, kg, cos_t, sin_t, bd, *, rope, tm):
    b, n, d = x.shape
    assert n % tm == 0
    tok = lambda width: pl.BlockSpec((1, tm, width), lambda i, t: (i, t, 0))
    const = lambda shape: pl.BlockSpec(shape, lambda i, t: tuple(0 for _ in shape))
    row = pl.BlockSpec((1, 1, d), lambda i, t: (i, 0, 0))
    outs = [(2 * GLA_QK, BF16), (GLA_WIDTH, BF16), (GLA_WIDTH, BF16), (DOWN_PAD, F32), (DIFF_WIDTH, BF16)]
    out_specs = [tok(wd) for wd, _ in outs]
    out_shape = [jax.ShapeDtypeStruct((b, n, wd), dt) for wd, dt in outs]
    out_specs.insert(4, pl.BlockSpec((1, DIFF_WIDTH, tm), lambda i, t: (i, 0, t)))
    out_shape.insert(4, jax.ShapeDtypeStruct((b, DIFF_WIDTH, n), BF16))
    out_specs.append(pl.BlockSpec((1, 1, DIFF_WIDTH, tm), lambda i, t: (i, t, 0, 0)))
    out_shape.append(jax.ShapeDtypeStruct((b, n // tm, DIFF_WIDTH, tm), BF16))
    return pl.pallas_call(
        functools.partial(_proj_kernel, rope=rope, sub=min(tm, PROJ_SUB_ROWS)),
        grid=(b, n // tm),
        in_specs=[tok(d), row, row, const((1, d)), const(w.shape), const((1, DIFF_WIDTH)),
                  const((1, DIFF_WIDTH)),
                  pl.BlockSpec((tm, LANES), lambda i, t: (t, 0)),
                  pl.BlockSpec((tm, LANES), lambda i, t: (t, 0)),
                  const(bd.shape)],
        out_specs=out_specs,
        out_shape=out_shape,
        compiler_params=pltpu.CompilerParams(dimension_semantics=("arbitrary", "arbitrary"),
                                             vmem_limit_bytes=VMEM_LIMIT),
        name="in_proj_rope" if rope else "in_proj_ctx",
    )(x, sh, sc, g1, w, qg, kg, cos_t, sin_t, bd)


def _split_bf16(x):
    hi = x.astype(BF16)
    return hi, (x - hi.astype(F32)).astype(BF16)


def _gla_prepare(qk_ref, dn_ref, gu, bias, tri, upper, z, prep):
    qin_s, qmid_s, kmid_s, kend_s, etot_s = prep
    nb, c = qk_ref.shape[0], qk_ref.shape[1]
    rows = nb * c
    down = dn_ref[...].reshape(rows, DOWN_PAD)
    dh, dl = _split_bf16(down)
    gh, gl = gu
    logits = _dot(dh, gh) + (_dot(dl, gh) + _dot(dh, gl)) + bias
    la = (jnp.minimum(logits, 0.0) - jnp.log(1.0 + jnp.exp(-jnp.abs(logits)))) * (1.0 / GLA_GATE_NORM)
    lh, ll = _split_bf16(la)
    cum_all = _dot(tri, lh) + _dot(tri, ll)

    for bi in range(nb):
        rs = slice(bi * c, (bi + 1) * c)
        cum = cum_all[rs]
        tot = cum[0:1] if upper else cum[c - 1:c]
        mid = cum[c // 2:c // 2 + 1]
        q = qk_ref[bi, :, 0:GLA_QK].astype(F32)
        k = qk_ref[bi, :, GLA_QK:2 * GLA_QK].astype(F32)
        qin_s[z, rs, :] = (q * jnp.exp(cum)).astype(BF16)
        qmid_s[z, rs, :] = q * jnp.exp(cum - mid)
        kmid_s[z, rs, :] = (k * jnp.exp(mid - cum)).astype(BF16)
        kend_s[z, rs, :] = (k * jnp.exp(tot - cum)).astype(BF16)
        etot_s[z, bi] = jnp.broadcast_to(jnp.exp(tot), etot_s.shape[2:])


def _gla_apply(prep, v_ref, st_ref, z, o_ref, head_lane, state_mask, tri_mask):
    qin_s, qmid_s, kmid_s, kend_s, etot_s = prep
    nb, c = v_ref.shape[0], v_ref.shape[1]
    q_mid = qmid_s[z]
    k_mid = kmid_s[z]
    v_all = v_ref[...].reshape(nb * c, GLA_WIDTH)
    outs = []
    for hd in range(GLA_HEADS):
        qh = jnp.where(head_lane == hd, q_mid, 0.0).astype(BF16)
        a = jnp.where(tri_mask, _dot_nt(qh, k_mid), 0.0).astype(BF16)
        outs.append(_dot(a, v_all[:, hd * GLA_DV:(hd + 1) * GLA_DV]))
    o_intra = jnp.concatenate(outs, axis=1)

    for bi in range(nb):
        rs = slice(bi * c, (bi + 1) * c)
        st = st_ref[bi, z]
        o_ref[bi] = o_intra[rs] + _dot_nt(qin_s[z, rs, :], st.astype(BF16))
        ut = _dot_tn(v_ref[bi], kend_s[z, rs, :])
        st_ref[bi, z] = etot_s[z, bi, 0:1, :] * st + jnp.where(state_mask, ut, 0.0)


def _gla_kernel(qk_f, dn_f, v_f, qk_b, dn_b, v_b, gu_ref, bias_ref, s0_ref,
                of_ref, ob_ref, sout_ref, st_ref, *prep):
    s = pl.program_id(0)
    nb, c = v_f.shape[0], v_f.shape[1]
    rows = nb * c

    @pl.when(s == 0)
    def _():
        st_ref[...] = s0_ref[...]
        for ref in prep[:-1]:
            ref[...] = jnp.zeros_like(ref)
        prep[-1][...] = jnp.ones_like(prep[-1])

    r = lax.broadcasted_iota(jnp.int32, (rows, rows), 0)
    cc = lax.broadcasted_iota(jnp.int32, (rows, rows), 1)
    same = (r // c) == (cc // c)
    lower = same & (r >= cc)
    upper = same & (r <= cc)
    head_lane = lax.broadcasted_iota(jnp.int32, (rows, GLA_QK), 1) // GLA_DK
    sr = lax.broadcasted_iota(jnp.int32, (GLA_WIDTH, GLA_QK), 0) // GLA_DV
    sl = lax.broadcasted_iota(jnp.int32, (GLA_WIDTH, GLA_QK), 1) // GLA_DK
    state_mask = sr == sl

    _gla_apply(prep, v_f, st_ref, 0, of_ref, head_lane, state_mask, lower)
    _gla_apply(prep, v_b, st_ref, 1, ob_ref, head_lane, state_mask, upper)
    _gla_prepare(qk_f, dn_f, _split_bf16(gu_ref[0]), bias_ref[0], lower.astype(BF16), False, 0, prep)
    _gla_prepare(qk_b, dn_b, _split_bf16(gu_ref[1]), bias_ref[1], upper.astype(BF16), True, 1, prep)

    @pl.when(s == pl.num_programs(0) - 1)
    def _():
        sout_ref[...] = st_ref[...]


def _gla_call(gqk, gv, down, gu, bias, s0):
    b, n, _ = gqk.shape
    c = GLA_CHUNK
    nc = n // c
    rows = b * c
    prep_i = lambda s: jnp.minimum(s, nc - 1)
    appl_i = lambda s: jnp.maximum(s - 1, 0)
    spec = lambda width, idx, mirror: pl.BlockSpec(
        (b, c, width), (lambda s: (0, nc - 1 - idx(s), 0)) if mirror else (lambda s: (0, idx(s), 0)))
    state = pl.BlockSpec((b, 2, GLA_WIDTH, GLA_QK), lambda s: (0, 0, 0, 0))
    return pl.pallas_call(
        _gla_kernel,
        grid=(nc + 1,),
        in_specs=[spec(2 * GLA_QK, prep_i, False), spec(DOWN_PAD, prep_i, False), spec(GLA_WIDTH, appl_i, False),
                  spec(2 * GLA_QK, prep_i, True), spec(DOWN_PAD, prep_i, True), spec(GLA_WIDTH, appl_i, True),
                  pl.BlockSpec(gu.shape, lambda s: (0, 0, 0)),
                  pl.BlockSpec(bias.shape, lambda s: (0, 0, 0)),
                  state],
        out_specs=[spec(GLA_WIDTH, appl_i, False), spec(GLA_WIDTH, appl_i, True), state],
        out_shape=[jax.ShapeDtypeStruct((b, n, GLA_WIDTH), F32),
                   jax.ShapeDtypeStruct((b, n, GLA_WIDTH), F32),
                   jax.ShapeDtypeStruct((b, 2, GLA_WIDTH, GLA_QK), F32)],
        scratch_shapes=[pltpu.VMEM((b, 2, GLA_WIDTH, GLA_QK), F32),
                        pltpu.VMEM((2, rows, GLA_QK), BF16), pltpu.VMEM((2, rows, GLA_QK), F32),
                        pltpu.VMEM((2, rows, GLA_QK), BF16), pltpu.VMEM((2, rows, GLA_QK), BF16),
                        pltpu.VMEM((2, b, 8, GLA_QK), F32)],
        compiler_params=pltpu.CompilerParams(dimension_semantics=("arbitrary",),
                                             vmem_limit_bytes=VMEM_LIMIT),
        name="gla_bidir",
    )(gqk, down, gv, gqk, down, gv, gu, bias, s0)


def _block_diag_queries(qt):
    row = lax.broadcasted_iota(jnp.int32, qt.shape, 0)
    zero = jnp.zeros_like(qt)
    return jnp.concatenate([jnp.where(row < DIFF_DH, qt, zero),
                            jnp.where(row >= DIFF_DH, qt, zero)], axis=1)


def _attn_finish(lq_ref, lk_ref, g_ref, o_ref, acc, l, tq, lam_init):
    lqk = jnp.sum(lq_ref[...] * lk_ref[...], axis=1, keepdims=True)
    e = jnp.exp(lqk)
    lam = e[0:1] - e[1:2] + lam_init
    acc = acc * (1.0 / l)
    ot = acc[:, 0:tq] - lam * acc[:, tq:2 * tq]
    ms = jnp.mean(ot * ot, axis=0, keepdims=True)
    y = ot * lax.rsqrt(ms + EPS) * (g_ref[...] * (1.0 - lam_init))
    o_ref[0] = y.T.astype(BF16)


def _attn_online_kernel(lq_ref, lk_ref, g_ref, qt_ref, kc_ref, kl_ref, vtc_ref, vtl_ref, o_ref,
                        m_ref, l_ref, acc_ref, *, lam_init):
    tq = qt_ref.shape[2]
    q_bd = _block_diag_queries(qt_ref[0])

    s = _dot(kc_ref[0], q_bd)
    m0 = jnp.max(s, axis=0, keepdims=True)
    p = jnp.exp2(s - m0)
    m_ref[...] = m0
    l_ref[...] = jnp.sum(p, axis=0, keepdims=True)
    acc_ref[...] = _dot(vtc_ref[0], p.astype(BF16))

    tk = vtl_ref.shape[3]

    def body(j, carry):
        s = _dot(kl_ref[0, pl.ds(pl.multiple_of(j * tk, tk), tk), :], q_bd)
        m_old = m_ref[...]
        m_new = jnp.maximum(m_old, jnp.max(s, axis=0, keepdims=True))
        alpha = jnp.exp2(m_old - m_new)
        p = jnp.exp2(s - m_new)
        m_ref[...] = m_new
        l_ref[...] = alpha * l_ref[...] + jnp.sum(p, axis=0, keepdims=True)
        acc_ref[...] = alpha * acc_ref[...] + _dot(vtl_ref[0, j], p.astype(BF16))
        return carry

    lax.fori_loop(0, vtl_ref.shape[1], body, 0)
    _attn_finish(lq_ref, lk_ref, g_ref, o_ref, acc_ref[...], l_ref[...], tq, lam_init)


def _attn_bounded_kernel(lq_ref, lk_ref, g_ref, qt_ref, kc_ref, kl_ref, vtc_ref, vtl_ref, o_ref,
                         s_ref, *, lam_init):
    tq = qt_ref.shape[2]
    tk = vtl_ref.shape[3]
    q_bd = _block_diag_queries(qt_ref[0])
    n_ctx = kc_ref.shape[1]
    tiles = [(n_ctx, lambda: kc_ref[0], lambda: vtc_ref[0])]
    for j in range(vtl_ref.shape[1]):
        tiles.append((tk, lambda j=j: kl_ref[0, j * tk:(j + 1) * tk, :], lambda j=j: vtl_ref[0, j]))

    s_ref[0, 0:n_ctx, :] = _dot(tiles[0][1](), q_bd)
    l = jnp.zeros((1, 2 * tq), F32)
    acc = jnp.zeros((DIFF_DV, 2 * tq), F32)
    for t, (rows, _, vt) in enumerate(tiles):
        if t + 1 < len(tiles):
            nxt = tiles[t + 1]
            s_ref[(t + 1) % 2, 0:nxt[0], :] = _dot(nxt[1](), q_bd)
        p = jnp.exp2(s_ref[t % 2, 0:rows, :])
        l = l + jnp.sum(p, axis=0, keepdims=True)
        acc = acc + _dot(vt(), p.astype(BF16))
    _attn_finish(lq_ref, lk_ref, g_ref, o_ref, acc, l, tq, lam_init)


def _attn_call(body, lq, lk, g_col, qt, k_ctx, k_lat, vt_ctx, vt_lat, *, lam_init, tq):
    b, _, n = qt.shape
    n_ctx = k_ctx.shape[1]
    nch, tk = vt_lat.shape[1], vt_lat.shape[3]
    const = lambda shape: pl.BlockSpec(shape, lambda bi, h, qi: tuple(0 for _ in shape))
    return pl.pallas_call(
        functools.partial(body, lam_init=lam_init),
        grid=(b, DIFF_HEADS, n // tq),
        in_specs=[const(lq.shape), const(lk.shape), const(g_col.shape),
                  pl.BlockSpec((1, DIFF_DV, tq), lambda bi, h, qi: (bi, h, qi)),
                  pl.BlockSpec((1, n_ctx, DIFF_DV), lambda bi, h, qi: (bi, 0, h)),
                  pl.BlockSpec((1, n, DIFF_DV), lambda bi, h, qi: (bi, 0, h)),
                  pl.BlockSpec((1, DIFF_DV, n_ctx), lambda bi, h, qi: (bi, h, 0)),
                  pl.BlockSpec((1, nch, DIFF_DV, tk), lambda bi, h, qi: (bi, 0, h, 0))],
        out_specs=pl.BlockSpec((1, tq, DIFF_DV), lambda bi, h, qi: (bi, qi, h)),
        out_shape=jax.ShapeDtypeStruct((b, n, DIFF_WIDTH), BF16),
        scratch_shapes=(
            [pltpu.VMEM((2, max(tk, n_ctx), 2 * tq), F32)] if body is _attn_bounded_kernel else
            [pltpu.VMEM((1, 2 * tq), F32), pltpu.VMEM((1, 2 * tq), F32), pltpu.VMEM((DIFF_DV, 2 * tq), F32)]),
        compiler_params=pltpu.CompilerParams(
            dimension_semantics=("arbitrary", "arbitrary", "arbitrary"),
            vmem_limit_bytes=VMEM_LIMIT),
        name=body.__name__.strip("_"),
    )(lq, lk, g_col, qt, k_ctx, k_lat, vt_ctx, vt_lat)


def _out_ffn_kernel(x_ref, of_ref, ob_ref, r_ref, od_ref, gt1_ref, sh2_ref, sc2_ref, gt2_ref,
                    gg_ref, g2_ref, wo_ref, wgu_ref, wd_ref, o_ref, acc_ref):
    og = of_ref[0] + ob_ref[0]
    r = r_ref[0].astype(F32)
    parts = []
    for hd in range(GLA_HEADS):
        sl = slice(hd * GLA_DV, (hd + 1) * GLA_DV)
        oh = og[:, sl]
        ms = jnp.mean(oh * oh, axis=-1, keepdims=True)
        parts.append((oh * lax.rsqrt(ms + EPS) * gg_ref[...] * _silu(r[:, sl])).astype(BF16))
    gla = jnp.concatenate(parts, axis=1)
    mix = _dot(gla, wo_ref[0:GLA_WIDTH, :]) + _dot(od_ref[0], wo_ref[GLA_WIDTH:, :])
    x1 = x_ref[0] + gt1_ref[0] * mix
    ms = jnp.mean(x1 * x1, axis=-1, keepdims=True)
    h = x1 * lax.rsqrt(ms + EPS) * g2_ref[...]
    hb = (h * (1.0 + sc2_ref[0]) + sh2_ref[0]).astype(BF16)

    fh = wd_ref.shape[0]
    acc_ref[...] = jnp.zeros_like(acc_ref)
    for lo in range(0, fh, FFN_CHUNK):
        gate = _dot(hb, wgu_ref[:, lo:lo + FFN_CHUNK])
        up = _dot(hb, wgu_ref[:, fh + lo:fh + lo + FFN_CHUNK])
        acc_ref[...] += _dot((_silu(gate) * up).astype(BF16), wd_ref[lo:lo + FFN_CHUNK, :])
    o_ref[0] = x1 + gt2_ref[0] * acc_ref[...]


def _out_ffn_call(x, o_f, o_b, gr, od, gt1, sh2, sc2, gt2, gg, g2, wo, wgu, wd, *, tm):
    b, n, d = x.shape
    tok = lambda width: pl.BlockSpec((1, tm, width), lambda i, t: (i, t, 0))
    row = pl.BlockSpec((1, 1, d), lambda i, t: (i, 0, 0))
    const = lambda shape: pl.BlockSpec(shape, lambda i, t: tuple(0 for _ in shape),
                                       pipeline_mode=pl.Buffered(1))
    return pl.pallas_call(
        _out_ffn_kernel,
        grid=(b, n // tm),
        in_specs=[tok(d), tok(GLA_WIDTH), tok(GLA_WIDTH), tok(GLA_WIDTH), tok(DIFF_WIDTH),
                  row, row, row, row, const(gg.shape), const(g2.shape),
                  const(wo.shape), const(wgu.shape), const(wd.shape)],
        out_specs=tok(d),
        out_shape=jax.ShapeDtypeStruct((b, n, d), F32),
        scratch_shapes=[pltpu.VMEM((tm, d), F32)],
        compiler_params=pltpu.CompilerParams(dimension_semantics=("arbitrary", "arbitrary"),
                                             vmem_limit_bytes=VMEM_LIMIT),
        name="out_proj_ffn",
    )(x, o_f, o_b, gr, od, gt1, sh2, sc2, gt2, gg, g2, wo, wgu, wd)


def _rope_tables(n):
    t = jnp.arange(n)
    pos = jnp.stack([(t // GRID_W).astype(F32), (t % GRID_W).astype(F32)], axis=1)
    inv_freq = ROPE_BASE ** (-jnp.arange(ROPE_HALF, dtype=F32) / ROPE_HALF)
    ang = pos[:, :, None] * inv_freq
    cos = jnp.cos(ang)[:, :, None, :]
    sin = jnp.sin(ang)[:, :, None, :] * jnp.array([-1.0, 1.0], F32)[None, None, :, None]
    cos = jnp.broadcast_to(cos, (n, 2, 2, ROPE_HALF)).reshape(n, DIFF_DH)
    sin = jnp.broadcast_to(sin, (n, 2, 2, ROPE_HALF)).reshape(n, DIFF_DH)
    return jnp.tile(cos, (1, LANES // DIFF_DH)), jnp.tile(sin, (1, LANES // DIFF_DH))


def _proj_weight(w_in):
    sizes = (GLA_QK, GLA_QK, GLA_WIDTH, GLA_WIDTH, 2 * GLA_GATE_RANK, DIFF_WIDTH, DIFF_WIDTH, DIFF_WIDTH)
    offs = [0]
    for s in sizes:
        offs.append(offs[-1] + s)
    part = lambda i: w_in[:, offs[i]:offs[i + 1]]
    pad = jnp.zeros((w_in.shape[0], DOWN_PAD - 2 * GLA_GATE_RANK), w_in.dtype)
    return jnp.concatenate([part(0), part(1), part(2), part(3), part(5), part(6), part(7), part(4), pad],
                           axis=1).astype(BF16)


def _tile_rows(n, pref):
    for t in pref:
        if n % t == 0:
            return t
    raise ValueError(f"no row tile for {n}")


def kernel(x, c, ctx, c_ctx, w_mod, b_mod, norm1_g, w_in, gla_gate_up, gla_gate_bias, gla_norm_g,
           diff_q_norm_g, diff_k_norm_g, diff_lambda_q, diff_lambda_k, diff_norm_g, w_out, norm2_g,
           w_ffn_in, w_ffn_out):
    assert w_mod.shape[0] == 1, "single-layer stack"
    b, n, d = x.shape
    n_ctx = ctx.shape[1]
    lam_init = 0.8 - 0.6 * math.exp(-0.3 * 0)

    cond = jnp.concatenate([c, c_ctx[None, :], jnp.zeros((-(b + 1) % 8, d), F32)], axis=0)
    mod = _mod_call(cond, w_mod[0], b_mod[0][None, :])
    sh1, sc1, gt1, sh2, sc2, gt2 = [mod[:, i * d:(i + 1) * d] for i in range(6)]
    lat = lambda m: m[0:b, None, :]
    ctx_rows = lambda m: jnp.broadcast_to(m[b:b + 1, None, :], (b, 1, d))

    w_p = _proj_weight(w_in[0])
    g1 = norm1_g[0][None, :]
    qg = jnp.tile(diff_q_norm_g[0], DIFF_WIDTH // DIFF_DH)[None, :]
    kg = jnp.tile(diff_k_norm_g[0], DIFF_WIDTH // DIFF_DH)[None, :]
    gid = jnp.arange(DIFF_WIDTH) // DIFF_DH
    bd = (gid[:, None] == gid[None, :]).astype(BF16)
    cos_t, sin_t = _rope_tables(n)

    tm_c = _tile_rows(n_ctx, (256, 128, 64))
    tm_x = _tile_rows(n, (512, 256, 128, 64))
    pc = _proj_call(ctx, ctx_rows(sh1), ctx_rows(sc1), g1, w_p, qg, kg, cos_t[:n_ctx], sin_t[:n_ctx], bd,
                    rope=False, tm=tm_c)
    px = _proj_call(x, lat(sh1), lat(sc1), g1, w_p, qg, kg, cos_t, sin_t, bd, rope=True, tm=tm_x)
    gqk_c, gv_c, _, down_c, _, dk_c, vt_c = pc
    gqk, gv, gr, down, qt, dk, vt_lat = px

    gu = jnp.zeros((2, DOWN_PAD, GLA_QK), F32)
    for z in range(2):
        gu = gu.at[z, z * GLA_GATE_RANK:(z + 1) * GLA_GATE_RANK, :].set(gla_gate_up[0, z])
    bias = gla_gate_bias[0][:, None, :]
    zero_state = jnp.zeros((b, 2, GLA_WIDTH, GLA_QK), F32)
    _, _, s_ctx = _gla_call(gqk_c, gv_c, down_c, gu, bias, zero_state)
    o_f, o_b, _ = _gla_call(gqk, gv, down, gu, bias, s_ctx)

    tq = _tile_rows(n, (512, 256, 128))
    assert n_ctx == tm_c, "context keys are one attention tile"
    vt_ctx = vt_c[:, 0]
    score_bound = 8.0 * jnp.max(jnp.abs(diff_q_norm_g[0])) * jnp.max(jnp.abs(diff_k_norm_g[0]))
    attn_args = (diff_lambda_q[0], diff_lambda_k[0], diff_norm_g[0][:, None], qt, dk_c, dk, vt_ctx, vt_lat)
    od = lax.cond(
        score_bound <= SCORE_BOUND,
        lambda *a: _attn_call(_attn_bounded_kernel, *a, lam_init=lam_init, tq=tq),
        lambda *a: _attn_call(_attn_online_kernel, *a, lam_init=lam_init, tq=tq),
        *attn_args)

    assert w_ffn_out.shape[1] % FFN_CHUNK == 0
    return _out_ffn_call(x, o_f, o_b, gr, od, lat(gt1), lat(sh2), lat(sc2), lat(gt2),
                         gla_norm_g[0][None, :], norm2_g[0][None, :], w_out[0].astype(BF16),
                         w_ffn_in[0].astype(BF16), w_ffn_out[0].astype(BF16), tm=tm_x)
```

```python
import functools
import math

import jax
import jax.numpy as jnp
from jax import lax
from jax.experimental import pallas as pl
from jax.experimental.pallas import tpu as pltpu

F32 = jnp.float32
BF16 = jnp.bfloat16
HIGHEST = lax.Precision.HIGHEST

EPS = 1e-6
GRID_W = 64
GLA_HEADS = 4
GLA_DK = 64
GLA_DV = 128
GLA_QK = GLA_HEADS * GLA_DK
GLA_WIDTH = GLA_HEADS * GLA_DV
GLA_GATE_RANK = 16
GLA_GATE_NORM = 16.0
GLA_CHUNK = 64
DIFF_HEADS = 4
DIFF_DH = 64
DIFF_DV = 128
DIFF_WIDTH = DIFF_HEADS * DIFF_DV
ROPE_BASE = 10000.0
ROPE_AXIS_DIM = DIFF_DH // 2
ROPE_HALF = ROPE_AXIS_DIM // 2
LANES = 128
MXU_TILE = 256
DOWN_PAD = LANES
LOG2_E = math.log2(math.e)
SCORE_BOUND = 75.0

VMEM_LIMIT = 56 * 1024 * 1024
PROJ_SUB_ROWS = 512
FFN_CHUNK = 256
ATTN_SUB_Q = 512


def _dot(a, b, precision=None):
    return jnp.dot(a, b, preferred_element_type=F32, precision=precision)


def _dot_nt(a, b):
    return lax.dot_general(a, b, (((1,), (1,)), ((), ())), preferred_element_type=F32)


def _dot_tn(a, b):
    return lax.dot_general(a, b, (((0,), (0,)), ((), ())), preferred_element_type=F32)


def _silu(x):
    return x / (1.0 + jnp.exp(-x))


def _mod_kernel(c_ref, w_ref, b_ref, o_ref):
    o_ref[...] = _dot(_silu(c_ref[...]), w_ref[...], HIGHEST) + b_ref[...]


def _mod_call(cond, w_mod, b_mod):
    rows, d = cond.shape
    cols = w_mod.shape[1]
    bn = cols // 4
    return pl.pallas_call(
        _mod_kernel,
        grid=(cols // bn,),
        in_specs=[pl.BlockSpec((rows, d), lambda j: (0, 0)),
                  pl.BlockSpec((d, bn), lambda j: (0, j)),
                  pl.BlockSpec((1, bn), lambda j: (0, j))],
        out_specs=pl.BlockSpec((rows, bn), lambda j: (0, j)),
        out_shape=jax.ShapeDtypeStruct((rows, cols), F32),
        compiler_params=pltpu.CompilerParams(dimension_semantics=("arbitrary",),
                                             vmem_limit_bytes=VMEM_LIMIT),
        name="adaln_mod",
    )(cond, w_mod, b_mod)


_C_GQ, _C_GK, _C_GV, _C_GR, _C_DQ, _C_DK, _C_DV, _C_DOWN, _C_END = (
    0, 256, 512, 1024, 1536, 2048, 2560, 3072, 3072 + DOWN_PAD)


def _swap_halves(y):
    lane = lax.broadcasted_iota(jnp.int32, y.shape, 1)
    upper = (lane & ROPE_HALF) != 0
    return jnp.where(upper, pltpu.roll(y, ROPE_HALF, 1), pltpu.roll(y, LANES - ROPE_HALF, 1))


def _proj_kernel(x_ref, sh_ref, sc_ref, g1_ref, w_ref, qg_ref, kg_ref, cos_ref, sin_ref, bd_ref,
                 gqk_o, gv_o, gr_o, down_o, dq_o, dk_o, dv_o, *, rope, sub):
    for r0 in range(0, x_ref.shape[1], sub):
        _proj_rows(slice(r0, r0 + sub), x_ref, sh_ref, sc_ref, g1_ref, w_ref, qg_ref, kg_ref, cos_ref, sin_ref,
                   bd_ref, gqk_o, gv_o, gr_o, down_o, dq_o, dk_o, dv_o, rope)


def _proj_rows(rs, x_ref, sh_ref, sc_ref, g1_ref, w_ref, qg_ref, kg_ref, cos_ref, sin_ref, bd_ref,
               gqk_o, gv_o, gr_o, down_o, dq_o, dk_o, dv_o, rope):
    x = x_ref[0, rs, :]
    ms = jnp.mean(x * x, axis=-1, keepdims=True)
    h = x * lax.rsqrt(ms + EPS) * g1_ref[...]
    h = h * (1.0 + sc_ref[0]) + sh_ref[0]
    hb = h.astype(BF16)

    def mm(lo, hi):
        return _dot(hb, w_ref[:, lo:hi])

    gqk_o[0, rs, 0:GLA_QK] = (mm(_C_GQ, _C_GK) * (GLA_DK ** -0.5)).astype(BF16)
    gqk_o[0, rs, GLA_QK:2 * GLA_QK] = mm(_C_GK, _C_GV).astype(BF16)
    gv_o[0, rs, :] = mm(_C_GV, _C_GR).astype(BF16)
    gr_o[0, rs, :] = mm(_C_GR, _C_DQ).astype(BF16)
    dv = mm(_C_DV, _C_DOWN)
    for s in range(DIFF_WIDTH // LANES):
        dv_o[0, 0, s * LANES:(s + 1) * LANES, rs] = dv[:, s * LANES:(s + 1) * LANES].T.astype(BF16)
    down_o[0, rs, :] = mm(_C_DOWN, _C_END)

    def qk_norm(y, g_ref, scale, store):
        y2 = (y * y).astype(BF16)
        bw = bd_ref.shape[0]
        ss = jnp.concatenate([_dot(y2[:, lo:lo + bw], bd_ref[...]) for lo in range(0, DIFF_WIDTH, bw)], axis=1)
        yn = y * lax.rsqrt(ss * (1.0 / DIFF_DH) + EPS) * g_ref[...]
        for s in range(DIFF_WIDTH // LANES):
            slab = yn[:, s * LANES:(s + 1) * LANES]
            if rope:
                slab = slab * cos_ref[rs, :] + _swap_halves(slab) * sin_ref[rs, :]
            store(s, slab * scale)

    def store_q(s, slab):
        dq_o[0, s * LANES:(s + 1) * LANES, rs] = slab.T.astype(BF16)

    def store_k(s, slab):
        dk_o[0, rs, s * LANES:(s + 1) * LANES] = slab.astype(BF16)

    qk_norm(mm(_C_DQ, _C_DK), qg_ref, DIFF_DH ** -0.5 * LOG2_E, store_q)
    qk_norm(mm(_C_DK, _C_DV), kg_ref, 1.0, store_k)


def _proj_call(x, sh, sc, g1, w, qg, kg, cos_t, sin_t, bd, *, rope, tm):
    b, n, d = x.shape
    assert n % tm == 0
    tok = lambda width: pl.BlockSpec((1, tm, width), lambda i, t: (i, t, 0))
    const = lambda shape: pl.BlockSpec(shape, lambda i, t: tuple(0 for _ in shape))
    row = pl.BlockSpec((1, 1, d), lambda i, t: (i, 0, 0))
    outs = [(2 * GLA_QK, BF16), (GLA_WIDTH, BF16), (GLA_WIDTH, BF16), (DOWN_PAD, F32), (DIFF_WIDTH, BF16)]
    out_specs = [tok(wd) for wd, _ in outs]
    out_shape = [jax.ShapeDtypeStruct((b, n, wd), dt) for wd, dt in outs]
    out_specs.insert(4, pl.BlockSpec((1, DIFF_WIDTH, tm), lambda i, t: (i, 0, t)))
    out_shape.insert(4, jax.ShapeDtypeStruct((b, DIFF_WIDTH, n), BF16))
    out_specs.append(pl.BlockSpec((1, 1, DIFF_WIDTH, tm), lambda i, t: (i, t, 0, 0)))
    out_shape.append(jax.ShapeDtypeStruct((b, n // tm, DIFF_WIDTH, tm), BF16))
    return pl.pallas_call(
        functools.partial(_proj_kernel, rope=rope, sub=min(tm, PROJ_SUB_ROWS)),
        grid=(b, n // tm),
        in_specs=[tok(d), row, row, const((1, d)), const(w.shape), const((1, DIFF_WIDTH)),
                  const((1, DIFF_WIDTH)),
                  pl.BlockSpec((tm, LANES), lambda i, t: (t, 0)),
                  pl.BlockSpec((tm, LANES), lambda i, t: (t, 0)),
                  const(bd.shape)],
        out_specs=out_specs,
        out_shape=out_shape,
        compiler_params=pltpu.CompilerParams(dimension_semantics=("arbitrary", "arbitrary"),
                                             vmem_limit_bytes=VMEM_LIMIT),
        name="in_proj_rope" if rope else "in_proj_ctx",
    )(x, sh, sc, g1, w, qg, kg, cos_t, sin_t, bd)


def _split_bf16(x):
    hi = x.astype(BF16)
    return hi, (x - hi.astype(F32)).astype(BF16)


def _gla_prepare(qk_ref, dn_ref, gu, bias, tri, upper, z, prep):
    qin_s, qmid_s, kmid_s, kend_s, etot_s = prep
    nb, c = qk_ref.shape[0], qk_ref.shape[1]
    rows = nb * c
    down = dn_ref[...].reshape(rows, DOWN_PAD)
    dh, dl = _split_bf16(down)
    gh, gl = gu
    logits = _dot(dh, gh) + (_dot(dl, gh) + _dot(dh, gl)) + bias
    la = (jnp.minimum(logits, 0.0) - jnp.log(1.0 + jnp.exp(-jnp.abs(logits)))) * (1.0 / GLA_GATE_NORM)
    lh, ll = _split_bf16(la)
    cum_all = _dot(tri, lh) + _dot(tri, ll)

    for bi in range(nb):
        rs = slice(bi * c, (bi + 1) * c)
        cum = cum_all[rs]
        tot = cum[0:1] if upper else cum[c - 1:c]
        mid = cum[c // 2:c // 2 + 1]
        q = qk_ref[bi, :, 0:GLA_QK].astype(F32)
        k = qk_ref[bi, :, GLA_QK:2 * GLA_QK].astype(F32)
        qin_s[z, rs, :] = (q * jnp.exp(cum)).astype(BF16)
        qmid_s[z, rs, :] = q * jnp.exp(cum - mid)
        kmid_s[z, rs, :] = (k * jnp.exp(mid - cum)).astype(BF16)
        kend_s[z, rs, :] = (k * jnp.exp(tot - cum)).astype(BF16)
        etot_s[z, bi] = jnp.broadcast_to(jnp.exp(tot), etot_s.shape[2:])


def _gla_apply(prep, v_ref, st_ref, z, o_ref, head_lane, state_mask, tri_mask):
    qin_s, qmid_s, kmid_s, kend_s, etot_s = prep
    nb, c = v_ref.shape[0], v_ref.shape[1]
    q_mid = qmid_s[z]
    k_mid = kmid_s[z]
    v_all = v_ref[...].reshape(nb * c, GLA_WIDTH)
    outs = []
    for hd in range(GLA_HEADS):
        qh = jnp.where(head_lane == hd, q_mid, 0.0).astype(BF16)
        a = jnp.where(tri_mask, _dot_nt(qh, k_mid), 0.0).astype(BF16)
        outs.append(_dot(a, v_all[:, hd * GLA_DV:(hd + 1) * GLA_DV]))
    o_intra = jnp.concatenate(outs, axis=1)

    for bi in range(nb):
        rs = slice(bi * c, (bi + 1) * c)
        st = st_ref[bi, z]
        o_ref[bi] = o_intra[rs] + _dot_nt(qin_s[z, rs, :], st.astype(BF16))
        ut = _dot_tn(v_ref[bi], kend_s[z, rs, :])
        st_ref[bi, z] = etot_s[z, bi, 0:1, :] * st + jnp.where(state_mask, ut, 0.0)


def _gla_kernel(qk_f, dn_f, v_f, qk_b, dn_b, v_b, gu_ref, bias_ref, s0_ref,
                of_ref, ob_ref, sout_ref, st_ref, *prep):
    s = pl.program_id(0)
    nb, c = v_f.shape[0], v_f.shape[1]
    rows = nb * c

    @pl.when(s == 0)
    def _():
        st_ref[...] = s0_ref[...]
        for ref in prep[:-1]:
            ref[...] = jnp.zeros_like(ref)
        prep[-1][...] = jnp.ones_like(prep[-1])

    r = lax.broadcasted_iota(jnp.int32, (rows, rows), 0)
    cc = lax.broadcasted_iota(jnp.int32, (rows, rows), 1)
    same = (r // c) == (cc // c)
    lower = same & (r >= cc)
    upper = same & (r <= cc)
    head_lane = lax.broadcasted_iota(jnp.int32, (rows, GLA_QK), 1) // GLA_DK
    sr = lax.broadcasted_iota(jnp.int32, (GLA_WIDTH, GLA_QK), 0) // GLA_DV
    sl = lax.broadcasted_iota(jnp.int32, (GLA_WIDTH, GLA_QK), 1) // GLA_DK
    state_mask = sr == sl

    _gla_apply(prep, v_f, st_ref, 0, of_ref, head_lane, state_mask, lower)
    _gla_apply(prep, v_b, st_ref, 1, ob_ref, head_lane, state_mask, upper)
    _gla_prepare(qk_f, dn_f, _split_bf16(gu_ref[0]), bias_ref[0], lower.astype(BF16), False, 0, prep)
    _gla_prepare(qk_b, dn_b, _split_bf16(gu_ref[1]), bias_ref[1], upper.astype(BF16), True, 1, prep)

    @pl.when(s == pl.num_programs(0) - 1)
    def _():
        sout_ref[...] = st_ref[...]


def _gla_call(gqk, gv, down, gu, bias, s0):
    b, n, _ = gqk.shape
    c = GLA_CHUNK
    nc = n // c
    rows = b * c
    prep_i = lambda s: jnp.minimum(s, nc - 1)
    appl_i = lambda s: jnp.maximum(s - 1, 0)
    spec = lambda width, idx, mirror: pl.BlockSpec(
        (b, c, width), (lambda s: (0, nc - 1 - idx(s), 0)) if mirror else (lambda s: (0, idx(s), 0)))
    state = pl.BlockSpec((b, 2, GLA_WIDTH, GLA_QK), lambda s: (0, 0, 0, 0))
    return pl.pallas_call(
        _gla_kernel,
        grid=(nc + 1,),
        in_specs=[spec(2 * GLA_QK, prep_i, False), spec(DOWN_PAD, prep_i, False), spec(GLA_WIDTH, appl_i, False),
                  spec(2 * GLA_QK, prep_i, True), spec(DOWN_PAD, prep_i, True), spec(GLA_WIDTH, appl_i, True),
                  pl.BlockSpec(gu.shape, lambda s: (0, 0, 0)),
                  pl.BlockSpec(bias.shape, lambda s: (0, 0, 0)),
                  state],
        out_specs=[spec(GLA_WIDTH, appl_i, False), spec(GLA_WIDTH, appl_i, True), state],
        out_shape=[jax.ShapeDtypeStruct((b, n, GLA_WIDTH), F32),
                   jax.ShapeDtypeStruct((b, n, GLA_WIDTH), F32),
                   jax.ShapeDtypeStruct((b, 2, GLA_WIDTH, GLA_QK), F32)],
        scratch_shapes=[pltpu.VMEM((b, 2, GLA_WIDTH, GLA_QK), F32),
                        pltpu.VMEM((2, rows, GLA_QK), BF16), pltpu.VMEM((2, rows, GLA_QK), F32),
                        pltpu.VMEM((2, rows, GLA_QK), BF16), pltpu.VMEM((2, rows, GLA_QK), BF16),
                        pltpu.VMEM((2, b, 8, GLA_QK), F32)],
        compiler_params=pltpu.CompilerParams(dimension_semantics=("arbitrary",),
                                             vmem_limit_bytes=VMEM_LIMIT),
        name="gla_bidir",
    )(gqk, down, gv, gqk, down, gv, gu, bias, s0)


def _block_diag_queries(qt):
    row = lax.broadcasted_iota(jnp.int32, qt.shape, 0)
    zero = jnp.zeros_like(qt)
    return jnp.concatenate([jnp.where(row < DIFF_DH, qt, zero),
                            jnp.where(row >= DIFF_DH, qt, zero)], axis=1)


def _attn_finish(lq_ref, lk_ref, g_ref, o_ref, qs, acc, l, lam_init):
    tq = qs.stop - qs.start
    lqk = jnp.sum(lq_ref[...] * lk_ref[...], axis=1, keepdims=True)
    e = jnp.exp(lqk)
    lam = e[0:1] - e[1:2] + lam_init
    acc = acc * (1.0 / l)
    ot = acc[:, 0:tq] - lam * acc[:, tq:2 * tq]
    ms = jnp.mean(ot * ot, axis=0, keepdims=True)
    y = ot * lax.rsqrt(ms + EPS) * (g_ref[...] * (1.0 - lam_init))
    o_ref[0, qs, :] = y.T.astype(BF16)


def _query_subtiles(qt_ref):
    tq = min(qt_ref.shape[2], ATTN_SUB_Q)
    return [(i, slice(i * tq, (i + 1) * tq)) for i in range(qt_ref.shape[2] // tq)]


def _attn_online_kernel(lq_ref, lk_ref, g_ref, qt_ref, kc_ref, kl_ref, vtc_ref, vtl_ref, o_ref,
                        m_ref, l_ref, acc_ref, *, lam_init):
    tk = vtl_ref.shape[3]
    for _, qs in _query_subtiles(qt_ref):
        q_bd = _block_diag_queries(qt_ref[0, :, qs])

        s = _dot(kc_ref[0], q_bd)
        m0 = jnp.max(s, axis=0, keepdims=True)
        p = jnp.exp2(s - m0)
        m_ref[...] = m0
        l_ref[...] = jnp.sum(p, axis=0, keepdims=True)
        acc_ref[...] = _dot(vtc_ref[0], p.astype(BF16))

        def body(j, carry, q_bd=q_bd):
            s = _dot(kl_ref[0, pl.ds(pl.multiple_of(j * tk, tk), tk), :], q_bd)
            m_old = m_ref[...]
            m_new = jnp.maximum(m_old, jnp.max(s, axis=0, keepdims=True))
            alpha = jnp.exp2(m_old - m_new)
            p = jnp.exp2(s - m_new)
            m_ref[...] = m_new
            l_ref[...] = alpha * l_ref[...] + jnp.sum(p, axis=0, keepdims=True)
            acc_ref[...] = alpha * acc_ref[...] + _dot(vtl_ref[0, j], p.astype(BF16))
            return carry

        lax.fori_loop(0, vtl_ref.shape[1], body, 0)
        _attn_finish(lq_ref, lk_ref, g_ref, o_ref, qs, acc_ref[...], l_ref[...], lam_init)


def _attn_bounded_kernel(lq_ref, lk_ref, g_ref, qt_ref, kc_ref, kl_ref, vtc_ref, vtl_ref, o_ref,
                         s_ref, *, lam_init):
    tk = vtl_ref.shape[3]
    n_ctx = kc_ref.shape[1]
    tiles = [(n_ctx, lambda: kc_ref[0], lambda: vtc_ref[0])]
    for j in range(vtl_ref.shape[1]):
        tiles.append((tk, lambda j=j: kl_ref[0, j * tk:(j + 1) * tk, :], lambda j=j: vtl_ref[0, j]))

    for qi, qs in _query_subtiles(qt_ref):
        tq = qs.stop - qs.start
        q_bd = _block_diag_queries(qt_ref[0, :, qs])
        s_ref[qi % 2, 0:n_ctx, :] = _dot(tiles[0][1](), q_bd)
        l = jnp.zeros((1, 2 * tq), F32)
        acc = jnp.zeros((DIFF_DV, 2 * tq), F32)
        for t, (rows, _, vt) in enumerate(tiles):
            if t + 1 < len(tiles):
                nxt = tiles[t + 1]
                s_ref[(t + 1 + qi) % 2, 0:nxt[0], :] = _dot(nxt[1](), q_bd)
            p = jnp.exp2(s_ref[(t + qi) % 2, 0:rows, :])
            l = l + jnp.sum(p, axis=0, keepdims=True)
            acc = acc + _dot(vt(), p.astype(BF16))
        _attn_finish(lq_ref, lk_ref, g_ref, o_ref, qs, acc, l, lam_init)


def _attn_call(body, lq, lk, g_col, qt, k_ctx, k_lat, vt_ctx, vt_lat, *, lam_init, tq):
    b, _, n = qt.shape
    n_ctx = k_ctx.shape[1]
    nch, tk = vt_lat.shape[1], vt_lat.shape[3]
    sq = min(tq, ATTN_SUB_Q)
    const = lambda shape: pl.BlockSpec(shape, lambda bi, h, qi: tuple(0 for _ in shape))
    return pl.pallas_call(
        functools.partial(body, lam_init=lam_init),
        grid=(b, DIFF_HEADS, n // tq),
        in_specs=[const(lq.shape), const(lk.shape), const(g_col.shape),
                  pl.BlockSpec((1, DIFF_DV, tq), lambda bi, h, qi: (bi, h, qi)),
                  pl.BlockSpec((1, n_ctx, DIFF_DV), lambda bi, h, qi: (bi, 0, h)),
                  pl.BlockSpec((1, n, DIFF_DV), lambda bi, h, qi: (bi, 0, h)),
                  pl.BlockSpec((1, DIFF_DV, n_ctx), lambda bi, h, qi: (bi, h, 0)),
                  pl.BlockSpec((1, nch, DIFF_DV, tk), lambda bi, h, qi: (bi, 0, h, 0))],
        out_specs=pl.BlockSpec((1, tq, DIFF_DV), lambda bi, h, qi: (bi, qi, h)),
        out_shape=jax.ShapeDtypeStruct((b, n, DIFF_WIDTH), BF16),
        scratch_shapes=(
            [pltpu.VMEM((2, max(tk, n_ctx), 2 * sq), F32)] if body is _attn_bounded_kernel else
            [pltpu.VMEM((1, 2 * sq), F32), pltpu.VMEM((1, 2 * sq), F32), pltpu.VMEM((DIFF_DV, 2 * sq), F32)]),
        compiler_params=pltpu.CompilerParams(
            dimension_semantics=("arbitrary", "arbitrary", "arbitrary"),
            vmem_limit_bytes=VMEM_LIMIT),
        name=body.__name__.strip("_"),
    )(lq, lk, g_col, qt, k_ctx, k_lat, vt_ctx, vt_lat)


def _out_ffn_kernel(x_ref, of_ref, ob_ref, r_ref, od_ref, gt1_ref, sh2_ref, sc2_ref, gt2_ref,
                    gg_ref, g2_ref, wo_ref, wgu_ref, wd_ref, o_ref, acc_ref):
    og = of_ref[0] + ob_ref[0]
    r = r_ref[0].astype(F32)
    parts = []
    for hd in range(GLA_HEADS):
        sl = slice(hd * GLA_DV, (hd + 1) * GLA_DV)
        oh = og[:, sl]
        ms = jnp.mean(oh * oh, axis=-1, keepdims=True)
        parts.append((oh * lax.rsqrt(ms + EPS) * gg_ref[...] * _silu(r[:, sl])).astype(BF16))
    gla = jnp.concatenate(parts, axis=1)
    mix = _dot(gla, wo_ref[0:GLA_WIDTH, :]) + _dot(od_ref[0], wo_ref[GLA_WIDTH:, :])
    x1 = x_ref[0] + gt1_ref[0] * mix
    ms = jnp.mean(x1 * x1, axis=-1, keepdims=True)
    h = x1 * lax.rsqrt(ms + EPS) * g2_ref[...]
    hb = (h * (1.0 + sc2_ref[0]) + sh2_ref[0]).astype(BF16)

    fh = wd_ref.shape[0]
    acc_ref[...] = jnp.zeros_like(acc_ref)
    for lo in range(0, fh, FFN_CHUNK):
        gate = _dot(hb, wgu_ref[:, lo:lo + FFN_CHUNK])
        up = _dot(hb, wgu_ref[:, fh + lo:fh + lo + FFN_CHUNK])
        acc_ref[...] += _dot((_silu(gate) * up).astype(BF16), wd_ref[lo:lo + FFN_CHUNK, :])
    o_ref[0] = x1 + gt2_ref[0] * acc_ref[...]


def _out_ffn_call(x, o_f, o_b, gr, od, gt1, sh2, sc2, gt2, gg, g2, wo, wgu, wd, *, tm):
    b, n, d = x.shape
    tok = lambda width: pl.BlockSpec((1, tm, width), lambda i, t: (i, t, 0))
    row = pl.BlockSpec((1, 1, d), lambda i, t: (i, 0, 0))
    const = lambda shape: pl.BlockSpec(shape, lambda i, t: tuple(0 for _ in shape),
                                       pipeline_mode=pl.Buffered(1))
    return pl.pallas_call(
        _out_ffn_kernel,
        grid=(b, n // tm),
        in_specs=[tok(d), tok(GLA_WIDTH), tok(GLA_WIDTH), tok(GLA_WIDTH), tok(DIFF_WIDTH),
                  row, row, row, row, const(gg.shape), const(g2.shape),
                  const(wo.shape), const(wgu.shape), const(wd.shape)],
        out_specs=tok(d),
        out_shape=jax.ShapeDtypeStruct((b, n, d), F32),
        scratch_shapes=[pltpu.VMEM((tm, d), F32)],
        compiler_params=pltpu.CompilerParams(dimension_semantics=("arbitrary", "arbitrary"),
                                             vmem_limit_bytes=VMEM_LIMIT),
        name="out_proj_ffn",
    )(x, o_f, o_b, gr, od, gt1, sh2, sc2, gt2, gg, g2, wo, wgu, wd)


def _rope_tables(n):
    t = jnp.arange(n)
    pos = jnp.stack([(t // GRID_W).astype(F32), (t % GRID_W).astype(F32)], axis=1)
    inv_freq = ROPE_BASE ** (-jnp.arange(ROPE_HALF, dtype=F32) / ROPE_HALF)
    ang = pos[:, :, None] * inv_freq
    cos = jnp.cos(ang)[:, :, None, :]
    sin = jnp.sin(ang)[:, :, None, :] * jnp.array([-1.0, 1.0], F32)[None, None, :, None]
    cos = jnp.broadcast_to(cos, (n, 2, 2, ROPE_HALF)).reshape(n, DIFF_DH)
    sin = jnp.broadcast_to(sin, (n, 2, 2, ROPE_HALF)).reshape(n, DIFF_DH)
    return jnp.tile(cos, (1, LANES // DIFF_DH)), jnp.tile(sin, (1, LANES // DIFF_DH))


def _proj_weight(w_in):
    sizes = (GLA_QK, GLA_QK, GLA_WIDTH, GLA_WIDTH, 2 * GLA_GATE_RANK, DIFF_WIDTH, DIFF_WIDTH, DIFF_WIDTH)
    offs = [0]
    for s in sizes:
        offs.append(offs[-1] + s)
    part = lambda i: w_in[:, offs[i]:offs[i + 1]]
    pad = jnp.zeros((w_in.shape[0], DOWN_PAD - 2 * GLA_GATE_RANK), w_in.dtype)
    return jnp.concatenate([part(0), part(1), part(2), part(3), part(5), part(6), part(7), part(4), pad],
                           axis=1).astype(BF16)


def _tile_rows(n, pref):
    for t in pref:
        if n % t == 0:
            return t
    raise ValueError(f"no row tile for {n}")


def kernel(x, c, ctx, c_ctx, w_mod, b_mod, norm1_g, w_in, gla_gate_up, gla_gate_bias, gla_norm_g,
           diff_q_norm_g, diff_k_norm_g, diff_lambda_q, diff_lambda_k, diff_norm_g, w_out, norm2_g,
           w_ffn_in, w_ffn_out):
    assert w_mod.shape[0] == 1, "single-layer stack"
    b, n, d = x.shape
    n_ctx = ctx.shape[1]
    lam_init = 0.8 - 0.6 * math.exp(-0.3 * 0)

    cond = jnp.concatenate([c, c_ctx[None, :], jnp.zeros((-(b + 1) % 8, d), F32)], axis=0)
    mod = _mod_call(cond, w_mod[0], b_mod[0][None, :])
    sh1, sc1, gt1, sh2, sc2, gt2 = [mod[:, i * d:(i + 1) * d] for i in range(6)]
    lat = lambda m: m[0:b, None, :]
    ctx_rows = lambda m: jnp.broadcast_to(m[b:b + 1, None, :], (b, 1, d))

    w_p = _proj_weight(w_in[0])
    g1 = norm1_g[0][None, :]
    qg = jnp.tile(diff_q_norm_g[0], DIFF_WIDTH // DIFF_DH)[None, :]
    kg = jnp.tile(diff_k_norm_g[0], DIFF_WIDTH // DIFF_DH)[None, :]
    gid = jnp.arange(MXU_TILE) // DIFF_DH
    bd = (gid[:, None] == gid[None, :]).astype(BF16)
    cos_t, sin_t = _rope_tables(n)

    tm_c = _tile_rows(n_ctx, (256, 128, 64))
    tm_x = _tile_rows(n, (512, 256, 128, 64))
    pc = _proj_call(ctx, ctx_rows(sh1), ctx_rows(sc1), g1, w_p, qg, kg, cos_t[:n_ctx], sin_t[:n_ctx], bd,
                    rope=False, tm=tm_c)
    px = _proj_call(x, lat(sh1), lat(sc1), g1, w_p, qg, kg, cos_t, sin_t, bd, rope=True, tm=tm_x)
    gqk_c, gv_c, _, down_c, _, dk_c, vt_c = pc
    gqk, gv, gr, down, qt, dk, vt_lat = px

    gu = jnp.zeros((2, DOWN_PAD, GLA_QK), F32)
    for z in range(2):
        gu = gu.at[z, z * GLA_GATE_RANK:(z + 1) * GLA_GATE_RANK, :].set(gla_gate_up[0, z])
    bias = gla_gate_bias[0][:, None, :]
    zero_state = jnp.zeros((b, 2, GLA_WIDTH, GLA_QK), F32)
    _, _, s_ctx = _gla_call(gqk_c, gv_c, down_c, gu, bias, zero_state)
    o_f, o_b, _ = _gla_call(gqk, gv, down, gu, bias, s_ctx)

    tq = _tile_rows(n, (2 * ATTN_SUB_Q, ATTN_SUB_Q, 256, 128))
    assert n_ctx == tm_c, "context keys are one attention tile"
    vt_ctx = vt_c[:, 0]
    score_bound = 8.0 * jnp.max(jnp.abs(diff_q_norm_g[0])) * jnp.max(jnp.abs(diff_k_norm_g[0]))
    attn_args = (diff_lambda_q[0], diff_lambda_k[0], diff_norm_g[0][:, None], qt, dk_c, dk, vt_ctx, vt_lat)
    od = lax.cond(
        score_bound <= SCORE_BOUND,
        lambda *a: _attn_call(_attn_bounded_kernel, *a, lam_init=lam_init, tq=tq),
        lambda *a: _attn_call(_attn_online_kernel, *a, lam_init=lam_init, tq=tq),
        *attn_args)

    assert w_ffn_out.shape[1] % FFN_CHUNK == 0
    return _out_ffn_call(x, o_f, o_b, gr, od, lat(gt1), lat(sh2), lat(sc2), lat(gt2),
                         gla_norm_g[0][None, :], norm2_g[0][None, :], w_out[0].astype(BF16),
                         w_ffn_in[0].astype(BF16), w_ffn_out[0].astype(BF16), tm=tm_x)
```

```python
import functools
import math

import jax
import jax.numpy as jnp
from jax import lax
from jax.experimental import pallas as pl
from jax.experimental.pallas import tpu as pltpu

F32 = jnp.float32
BF16 = jnp.bfloat16

EPS = 1e-6
GRID_W = 64
GLA_HEADS = 4
GLA_DK = 64
GLA_DV = 128
GLA_QK = GLA_HEADS * GLA_DK
GLA_WIDTH = GLA_HEADS * GLA_DV
GLA_GATE_RANK = 16
GLA_GATE_NORM = 16.0
GLA_CHUNK = 64
DIFF_HEADS = 4
DIFF_DH = 64
DIFF_DV = 128
DIFF_WIDTH = DIFF_HEADS * DIFF_DV
ROPE_BASE = 10000.0
ROPE_AXIS_DIM = DIFF_DH // 2
ROPE_HALF = ROPE_AXIS_DIM // 2
LANES = 128
SUBLANES = 8
MXU_TILE = 256
DOWN_PAD = LANES
LOG2_E = math.log2(math.e)
SCORE_BOUND = 75.0

VMEM_LIMIT = 56 * 1024 * 1024
FFN_CHUNK = 256
ATTN_SUB_Q = 512


def _dot(a, b):
    return jnp.dot(a, b, preferred_element_type=F32)


def _dot_nt(a, b):
    return lax.dot_general(a, b, (((1,), (1,)), ((), ())), preferred_element_type=F32)


def _dot_tn(a, b):
    return lax.dot_general(a, b, (((0,), (0,)), ((), ())), preferred_element_type=F32)


def _silu(x):
    return x / (1.0 + jnp.exp(-x))


def _split_bf16(x):
    hi = x.astype(BF16)
    return hi, (x - hi.astype(F32)).astype(BF16)


def _dot_split(a, b):
    ah, al = _split_bf16(a)
    bh, bl = _split_bf16(b)
    return _dot(ah, bh) + (_dot(al, bh) + _dot(ah, bl))


def _mod_kernel(c_ref, w_ref, b_ref, o_ref):
    o_ref[...] = _dot_split(_silu(c_ref[...]), w_ref[...]) + b_ref[...]


def _mod_call(cond, w_mod, b_mod):
    rows, d = cond.shape
    cols = w_mod.shape[1]
    bn = cols // 4
    return pl.pallas_call(
        _mod_kernel,
        grid=(cols // bn,),
        in_specs=[pl.BlockSpec((rows, d), lambda j: (0, 0)),
                  pl.BlockSpec((d, bn), lambda j: (0, j)),
                  pl.BlockSpec((1, bn), lambda j: (0, j))],
        out_specs=pl.BlockSpec((rows, bn), lambda j: (0, j)),
        out_shape=jax.ShapeDtypeStruct((rows, cols), F32),
        compiler_params=pltpu.CompilerParams(dimension_semantics=("arbitrary",),
                                             vmem_limit_bytes=VMEM_LIMIT),
        name="adaln_mod",
    )(cond, w_mod, b_mod)


_C_GQ, _C_GK, _C_GV, _C_GR, _C_DQ, _C_DK, _C_DV, _C_DOWN, _C_END = (
    0, 256, 512, 1024, 1536, 2048, 2560, 3072, 3072 + DOWN_PAD)


def _swap_halves(y):
    lane = lax.broadcasted_iota(jnp.int32, y.shape, 1)
    upper = (lane & ROPE_HALF) != 0
    return jnp.where(upper, pltpu.roll(y, ROPE_HALF, 1), pltpu.roll(y, LANES - ROPE_HALF, 1))


def _proj_kernel(x_ref, sh_ref, sc_ref, g1_ref, w_ref, qg_ref, kg_ref, cos_ref, sin_ref, bd_ref,
                 gqk_o, gv_o, gr_o, down_o, dq_o, dk_o, dv_o, *, rope):
    rs = slice(0, x_ref.shape[1])
    x = x_ref[0, rs, :]
    ms = jnp.mean(x * x, axis=-1, keepdims=True)
    h = x * lax.rsqrt(ms + EPS) * g1_ref[...]
    h = h * (1.0 + sc_ref[0]) + sh_ref[0]
    hb = h.astype(BF16)

    def mm(lo, hi):
        return _dot(hb, w_ref[:, lo:hi])

    gqk_o[0, rs, 0:GLA_QK] = (mm(_C_GQ, _C_GK) * (GLA_DK ** -0.5)).astype(BF16)
    gqk_o[0, rs, GLA_QK:2 * GLA_QK] = mm(_C_GK, _C_GV).astype(BF16)
    gv_o[0, rs, :] = mm(_C_GV, _C_GR).astype(BF16)
    gr_o[0, rs, :] = mm(_C_GR, _C_DQ).astype(BF16)
    dv = mm(_C_DV, _C_DOWN)
    for s in range(DIFF_WIDTH // LANES):
        dv_o[0, 0, s * LANES:(s + 1) * LANES, rs] = dv[:, s * LANES:(s + 1) * LANES].T.astype(BF16)
    down_o[0, rs, :] = mm(_C_DOWN, _C_END)

    def qk_norm(y, g_ref, scale, store):
        y2 = (y * y).astype(BF16)
        bw = bd_ref.shape[0]
        ss = jnp.concatenate([_dot(y2[:, lo:lo + bw], bd_ref[...]) for lo in range(0, DIFF_WIDTH, bw)], axis=1)
        yn = y * lax.rsqrt(ss * (1.0 / DIFF_DH) + EPS) * g_ref[...]
        for s in range(DIFF_WIDTH // LANES):
            slab = yn[:, s * LANES:(s + 1) * LANES]
            if rope:
                slab = slab * cos_ref[rs, :] + _swap_halves(slab) * sin_ref[rs, :]
            store(s, slab * scale)

    def store_q(s, slab):
        dq_o[0, s * LANES:(s + 1) * LANES, rs] = slab.T.astype(BF16)

    def store_k(s, slab):
        dk_o[0, rs, s * LANES:(s + 1) * LANES] = slab.astype(BF16)

    qk_norm(mm(_C_DQ, _C_DK), qg_ref, DIFF_DH ** -0.5 * LOG2_E, store_q)
    qk_norm(mm(_C_DK, _C_DV), kg_ref, 1.0, store_k)


def _proj_call(x, sh, sc, g1, w, qg, kg, cos_t, sin_t, bd, *, rope, tm):
    b, n, d = x.shape
    assert n % tm == 0
    tok = lambda width: pl.BlockSpec((1, tm, width), lambda i, t: (i, t, 0))
    const = lambda shape: pl.BlockSpec(shape, lambda i, t: tuple(0 for _ in shape))
    row = pl.BlockSpec((1, 1, d), lambda i, t: (i, 0, 0))
    outs = [(2 * GLA_QK, BF16), (GLA_WIDTH, BF16), (GLA_WIDTH, BF16), (DOWN_PAD, F32), (DIFF_WIDTH, BF16)]
    out_specs = [tok(wd) for wd, _ in outs]
    out_shape = [jax.ShapeDtypeStruct((b, n, wd), dt) for wd, dt in outs]
    out_specs.insert(4, pl.BlockSpec((1, DIFF_WIDTH, tm), lambda i, t: (i, 0, t)))
    out_shape.insert(4, jax.ShapeDtypeStruct((b, DIFF_WIDTH, n), BF16))
    out_specs.append(pl.BlockSpec((1, 1, DIFF_WIDTH, tm), lambda i, t: (i, t, 0, 0)))
    out_shape.append(jax.ShapeDtypeStruct((b, n // tm, DIFF_WIDTH, tm), BF16))
    return pl.pallas_call(
        functools.partial(_proj_kernel, rope=rope),
        grid=(b, n // tm),
        in_specs=[tok(d), row, row, const((1, d)), const(w.shape), const((1, DIFF_WIDTH)),
                  const((1, DIFF_WIDTH)),
                  pl.BlockSpec((tm, LANES), lambda i, t: (t, 0)),
                  pl.BlockSpec((tm, LANES), lambda i, t: (t, 0)),
                  const(bd.shape)],
        out_specs=out_specs,
        out_shape=out_shape,
        compiler_params=pltpu.CompilerParams(dimension_semantics=("arbitrary", "arbitrary"),
                                             vmem_limit_bytes=VMEM_LIMIT),
        name="in_proj_rope" if rope else "in_proj_ctx",
    )(x, sh, sc, g1, w, qg, kg, cos_t, sin_t, bd)


def _gla_prepare(qk_ref, dn_ref, gu, bias, tri, upper, z, prep):
    qin_s, qmid_s, kmid_s, kend_s, etot_s = prep
    nb, c = qk_ref.shape[0], qk_ref.shape[1]
    rows = nb * c
    down = dn_ref[...].reshape(rows, DOWN_PAD)
    dh, dl = _split_bf16(down)
    gh, gl = gu
    logits = _dot(dh, gh) + (_dot(dl, gh) + _dot(dh, gl)) + bias
    la = (jnp.minimum(logits, 0.0) - jnp.log(1.0 + jnp.exp(-jnp.abs(logits)))) * (1.0 / GLA_GATE_NORM)
    lh, ll = _split_bf16(la)
    cum_all = _dot(tri, lh) + _dot(tri, ll)

    for bi in range(nb):
        rs = slice(bi * c, (bi + 1) * c)
        cum = cum_all[rs]
        tot = cum[0:1] if upper else cum[c - 1:c]
        mid = cum[c // 2:c // 2 + 1]
        q = qk_ref[bi, :, 0:GLA_QK].astype(F32)
        k = qk_ref[bi, :, GLA_QK:2 * GLA_QK].astype(F32)
        qin_s[z, rs, :] = (q * jnp.exp(cum)).astype(BF16)
        qmid_s[z, rs, :] = q * jnp.exp(cum - mid)
        kmid_s[z, rs, :] = (k * jnp.exp(mid - cum)).astype(BF16)
        kend_s[z, rs, :] = (k * jnp.exp(tot - cum)).astype(BF16)
        etot_s[z, bi] = jnp.broadcast_to(jnp.exp(tot), etot_s.shape[2:])


def _gla_apply(prep, v_ref, st_ref, z, o_ref, head_lane, state_mask, tri_mask):
    qin_s, qmid_s, kmid_s, kend_s, etot_s = prep
    nb, c = v_ref.shape[0], v_ref.shape[1]
    q_mid = qmid_s[z]
    k_mid = kmid_s[z]
    v_all = v_ref[...].reshape(nb * c, GLA_WIDTH)
    outs = []
    for hd in range(GLA_HEADS):
        qh = jnp.where(head_lane == hd, q_mid, 0.0).astype(BF16)
        a = jnp.where(tri_mask, _dot_nt(qh, k_mid), 0.0).astype(BF16)
        outs.append(_dot(a, v_all[:, hd * GLA_DV:(hd + 1) * GLA_DV]))
    o_intra = jnp.concatenate(outs, axis=1)

    for bi in range(nb):
        rs = slice(bi * c, (bi + 1) * c)
        st = st_ref[bi, z]
        o_ref[bi] = o_intra[rs] + _dot_nt(qin_s[z, rs, :], st.astype(BF16))
        ut = _dot_tn(v_ref[bi], kend_s[z, rs, :])
        st_ref[bi, z] = etot_s[z, bi, 0:1, :] * st + jnp.where(state_mask, ut, 0.0)


def _gla_kernel(qk_f, dn_f, v_f, qk_b, dn_b, v_b, gu_ref, bias_ref, s0_ref,
                of_ref, ob_ref, sout_ref, st_ref, *prep):
    s = pl.program_id(0)
    nb, c = v_f.shape[0], v_f.shape[1]
    rows = nb * c

    @pl.when(s == 0)
    def _():
        st_ref[...] = s0_ref[...]
        for ref in prep[:-1]:
            ref[...] = jnp.zeros_like(ref)
        prep[-1][...] = jnp.ones_like(prep[-1])

    r = lax.broadcasted_iota(jnp.int32, (rows, rows), 0)
    cc = lax.broadcasted_iota(jnp.int32, (rows, rows), 1)
    same = (r // c) == (cc // c)
    lower = same & (r >= cc)
    upper = same & (r <= cc)
    head_lane = lax.broadcasted_iota(jnp.int32, (rows, GLA_QK), 1) // GLA_DK
    sr = lax.broadcasted_iota(jnp.int32, (GLA_WIDTH, GLA_QK), 0) // GLA_DV
    sl = lax.broadcasted_iota(jnp.int32, (GLA_WIDTH, GLA_QK), 1) // GLA_DK
    state_mask = sr == sl

    _gla_apply(prep, v_f, st_ref, 0, of_ref, head_lane, state_mask, lower)
    _gla_apply(prep, v_b, st_ref, 1, ob_ref, head_lane, state_mask, upper)
    _gla_prepare(qk_f, dn_f, _split_bf16(gu_ref[0]), bias_ref[0], lower.astype(BF16), False, 0, prep)
    _gla_prepare(qk_b, dn_b, _split_bf16(gu_ref[1]), bias_ref[1], upper.astype(BF16), True, 1, prep)

    @pl.when(s == pl.num_programs(0) - 1)
    def _():
        sout_ref[...] = st_ref[...]


def _gla_call(gqk, gv, down, gu, bias, s0):
    b, n, _ = gqk.shape
    c = GLA_CHUNK
    nc = n // c
    rows = b * c
    prep_i = lambda s: jnp.minimum(s, nc - 1)
    appl_i = lambda s: jnp.maximum(s - 1, 0)
    spec = lambda width, idx, mirror: pl.BlockSpec(
        (b, c, width), (lambda s: (0, nc - 1 - idx(s), 0)) if mirror else (lambda s: (0, idx(s), 0)))
    state = pl.BlockSpec((b, 2, GLA_WIDTH, GLA_QK), lambda s: (0, 0, 0, 0))
    return pl.pallas_call(
        _gla_kernel,
        grid=(nc + 1,),
        in_specs=[spec(2 * GLA_QK, prep_i, False), spec(DOWN_PAD, prep_i, False), spec(GLA_WIDTH, appl_i, False),
                  spec(2 * GLA_QK, prep_i, True), spec(DOWN_PAD, prep_i, True), spec(GLA_WIDTH, appl_i, True),
                  pl.BlockSpec(gu.shape, lambda s: (0, 0, 0)),
                  pl.BlockSpec(bias.shape, lambda s: (0, 0, 0)),
                  state],
        out_specs=[spec(GLA_WIDTH, appl_i, False), spec(GLA_WIDTH, appl_i, True), state],
        out_shape=[jax.ShapeDtypeStruct((b, n, GLA_WIDTH), F32),
                   jax.ShapeDtypeStruct((b, n, GLA_WIDTH), F32),
                   jax.ShapeDtypeStruct((b, 2, GLA_WIDTH, GLA_QK), F32)],
        scratch_shapes=[pltpu.VMEM((b, 2, GLA_WIDTH, GLA_QK), F32),
                        pltpu.VMEM((2, rows, GLA_QK), BF16), pltpu.VMEM((2, rows, GLA_QK), F32),
                        pltpu.VMEM((2, rows, GLA_QK), BF16), pltpu.VMEM((2, rows, GLA_QK), BF16),
                        pltpu.VMEM((2, b, SUBLANES, GLA_QK), F32)],
        compiler_params=pltpu.CompilerParams(dimension_semantics=("arbitrary",),
                                             vmem_limit_bytes=VMEM_LIMIT),
        name="gla_bidir",
    )(gqk, down, gv, gqk, down, gv, gu, bias, s0)


def _block_diag_queries(qt):
    row = lax.broadcasted_iota(jnp.int32, qt.shape, 0)
    zero = jnp.zeros_like(qt)
    return jnp.concatenate([jnp.where(row < DIFF_DH, qt, zero),
                            jnp.where(row >= DIFF_DH, qt, zero)], axis=1)


def _attn_finish(lq_ref, lk_ref, g_ref, o_ref, qs, acc, l, lam_init):
    tq = qs.stop - qs.start
    lqk = jnp.sum(lq_ref[...] * lk_ref[...], axis=1, keepdims=True)
    e = jnp.exp(lqk)
    lam = e[0:1] - e[1:2] + lam_init
    acc = acc * (1.0 / l)
    ot = acc[:, 0:tq] - lam * acc[:, tq:2 * tq]
    ms = jnp.mean(ot * ot, axis=0, keepdims=True)
    y = ot * lax.rsqrt(ms + EPS) * (g_ref[...] * (1.0 - lam_init))
    o_ref[0, qs, :] = y.T.astype(BF16)


def _query_subtiles(qt_ref):
    tq = min(qt_ref.shape[2], ATTN_SUB_Q)
    return [(i, slice(i * tq, (i + 1) * tq)) for i in range(qt_ref.shape[2] // tq)]


def _attn_online_kernel(lq_ref, lk_ref, g_ref, qt_ref, kc_ref, kl_ref, vtc_ref, vtl_ref, o_ref,
                        m_ref, l_ref, acc_ref, *, lam_init):
    tk = vtl_ref.shape[3]
    for _, qs in _query_subtiles(qt_ref):
        q_bd = _block_diag_queries(qt_ref[0, :, qs])

        s = _dot(kc_ref[0], q_bd)
        m0 = jnp.max(s, axis=0, keepdims=True)
        p = jnp.exp2(s - m0)
        m_ref[...] = m0
        l_ref[...] = jnp.sum(p, axis=0, keepdims=True)
        acc_ref[...] = _dot(vtc_ref[0], p.astype(BF16))

        def body(j, carry, q_bd=q_bd):
            s = _dot(kl_ref[0, pl.ds(pl.multiple_of(j * tk, tk), tk), :], q_bd)
            m_old = m_ref[...]
            m_new = jnp.maximum(m_old, jnp.max(s, axis=0, keepdims=True))
            alpha = jnp.exp2(m_old - m_new)
            p = jnp.exp2(s - m_new)
            m_ref[...] = m_new
            l_ref[...] = alpha * l_ref[...] + jnp.sum(p, axis=0, keepdims=True)
            acc_ref[...] = alpha * acc_ref[...] + _dot(vtl_ref[0, j], p.astype(BF16))
            return carry

        lax.fori_loop(0, vtl_ref.shape[1], body, 0)
        _attn_finish(lq_ref, lk_ref, g_ref, o_ref, qs, acc_ref[...], l_ref[...], lam_init)


def _attn_bounded_kernel(lq_ref, lk_ref, g_ref, qt_ref, kc_ref, kl_ref, vtc_ref, vtl_ref, o_ref,
                         s_ref, *, lam_init):
    tk = vtl_ref.shape[3]
    n_ctx = kc_ref.shape[1]
    tiles = [(n_ctx, lambda: kc_ref[0], lambda: vtc_ref[0])]
    for j in range(vtl_ref.shape[1]):
        tiles.append((tk, lambda j=j: kl_ref[0, j * tk:(j + 1) * tk, :], lambda j=j: vtl_ref[0, j]))

    for qi, qs in _query_subtiles(qt_ref):
        tq = qs.stop - qs.start
        q_bd = _block_diag_queries(qt_ref[0, :, qs])
        s_ref[qi % 2, 0:n_ctx, :] = _dot(tiles[0][1](), q_bd)
        l = jnp.zeros((1, 2 * tq), F32)
        acc = jnp.zeros((DIFF_DV, 2 * tq), F32)
        for t, (rows, _, vt) in enumerate(tiles):
            if t + 1 < len(tiles):
                nxt = tiles[t + 1]
                s_ref[(t + 1 + qi) % 2, 0:nxt[0], :] = _dot(nxt[1](), q_bd)
            p = jnp.exp2(s_ref[(t + qi) % 2, 0:rows, :])
            l = l + jnp.sum(p, axis=0, keepdims=True)
            acc = acc + _dot(vt(), p.astype(BF16))
        _attn_finish(lq_ref, lk_ref, g_ref, o_ref, qs, acc, l, lam_init)


def _attn_call(body, lq, lk, g_col, qt, k_ctx, k_lat, vt_ctx, vt_lat, *, lam_init, tq):
    b, _, n = qt.shape
    n_ctx = k_ctx.shape[1]
    nch, tk = vt_lat.shape[1], vt_lat.shape[3]
    sq = min(tq, ATTN_SUB_Q)
    const = lambda shape: pl.BlockSpec(shape, lambda bi, h, qi: tuple(0 for _ in shape))
    return pl.pallas_call(
        functools.partial(body, lam_init=lam_init),
        grid=(b, DIFF_HEADS, n // tq),
        in_specs=[const(lq.shape), const(lk.shape), const(g_col.shape),
                  pl.BlockSpec((1, DIFF_DV, tq), lambda bi, h, qi: (bi, h, qi)),
                  pl.BlockSpec((1, n_ctx, DIFF_DV), lambda bi, h, qi: (bi, 0, h)),
                  pl.BlockSpec((1, n, DIFF_DV), lambda bi, h, qi: (bi, 0, h)),
                  pl.BlockSpec((1, DIFF_DV, n_ctx), lambda bi, h, qi: (bi, h, 0)),
                  pl.BlockSpec((1, nch, DIFF_DV, tk), lambda bi, h, qi: (bi, 0, h, 0))],
        out_specs=pl.BlockSpec((1, tq, DIFF_DV), lambda bi, h, qi: (bi, qi, h)),
        out_shape=jax.ShapeDtypeStruct((b, n, DIFF_WIDTH), BF16),
        scratch_shapes=(
            [pltpu.VMEM((2, max(tk, n_ctx), 2 * sq), F32)] if body is _attn_bounded_kernel else
            [pltpu.VMEM((1, 2 * sq), F32), pltpu.VMEM((1, 2 * sq), F32), pltpu.VMEM((DIFF_DV, 2 * sq), F32)]),
        compiler_params=pltpu.CompilerParams(
            dimension_semantics=("arbitrary", "arbitrary", "arbitrary"),
            vmem_limit_bytes=VMEM_LIMIT),
        name=body.__name__.strip("_"),
    )(lq, lk, g_col, qt, k_ctx, k_lat, vt_ctx, vt_lat)


def _out_ffn_kernel(x_ref, of_ref, ob_ref, r_ref, od_ref, gt1_ref, sh2_ref, sc2_ref, gt2_ref,
                    gg_ref, g2_ref, wo_ref, wgu_ref, wd_ref, o_ref, acc_ref):
    og = of_ref[0] + ob_ref[0]
    r = r_ref[0].astype(F32)
    parts = []
    for hd in range(GLA_HEADS):
        sl = slice(hd * GLA_DV, (hd + 1) * GLA_DV)
        oh = og[:, sl]
        ms = jnp.mean(oh * oh, axis=-1, keepdims=True)
        parts.append((oh * lax.rsqrt(ms + EPS) * gg_ref[...] * _silu(r[:, sl])).astype(BF16))
    gla = jnp.concatenate(parts, axis=1)
    mix = _dot(gla, wo_ref[0:GLA_WIDTH, :]) + _dot(od_ref[0], wo_ref[GLA_WIDTH:, :])
    x1 = x_ref[0] + gt1_ref[0] * mix
    ms = jnp.mean(x1 * x1, axis=-1, keepdims=True)
    h = x1 * lax.rsqrt(ms + EPS) * g2_ref[...]
    hb = (h * (1.0 + sc2_ref[0]) + sh2_ref[0]).astype(BF16)

    fh = wd_ref.shape[0]
    acc_ref[...] = jnp.zeros_like(acc_ref)
    for lo in range(0, fh, FFN_CHUNK):
        gate = _dot(hb, wgu_ref[:, lo:lo + FFN_CHUNK])
        up = _dot(hb, wgu_ref[:, fh + lo:fh + lo + FFN_CHUNK])
        acc_ref[...] += _dot((_silu(gate) * up).astype(BF16), wd_ref[lo:lo + FFN_CHUNK, :])
    o_ref[0] = x1 + gt2_ref[0] * acc_ref[...]


def _out_ffn_call(x, o_f, o_b, gr, od, gt1, sh2, sc2, gt2, gg, g2, wo, wgu, wd, *, tm):
    b, n, d = x.shape
    tok = lambda width: pl.BlockSpec((1, tm, width), lambda i, t: (i, t, 0))
    row = pl.BlockSpec((1, 1, d), lambda i, t: (i, 0, 0))
    const = lambda shape: pl.BlockSpec(shape, lambda i, t: tuple(0 for _ in shape),
                                       pipeline_mode=pl.Buffered(1))
    return pl.pallas_call(
        _out_ffn_kernel,
        grid=(b, n // tm),
        in_specs=[tok(d), tok(GLA_WIDTH), tok(GLA_WIDTH), tok(GLA_WIDTH), tok(DIFF_WIDTH),
                  row, row, row, row, const(gg.shape), const(g2.shape),
                  const(wo.shape), const(wgu.shape), const(wd.shape)],
        out_specs=tok(d),
        out_shape=jax.ShapeDtypeStruct((b, n, d), F32),
        scratch_shapes=[pltpu.VMEM((tm, d), F32)],
        compiler_params=pltpu.CompilerParams(dimension_semantics=("arbitrary", "arbitrary"),
                                             vmem_limit_bytes=VMEM_LIMIT),
        name="out_proj_ffn",
    )(x, o_f, o_b, gr, od, gt1, sh2, sc2, gt2, gg, g2, wo, wgu, wd)


def _rope_tables(n):
    t = jnp.arange(n)
    pos = jnp.stack([(t // GRID_W).astype(F32), (t % GRID_W).astype(F32)], axis=1)
    inv_freq = ROPE_BASE ** (-jnp.arange(ROPE_HALF, dtype=F32) / ROPE_HALF)
    ang = pos[:, :, None] * inv_freq
    cos = jnp.cos(ang)[:, :, None, :]
    sin = jnp.sin(ang)[:, :, None, :] * jnp.array([-1.0, 1.0], F32)[None, None, :, None]
    cos = jnp.broadcast_to(cos, (n, 2, 2, ROPE_HALF)).reshape(n, DIFF_DH)
    sin = jnp.broadcast_to(sin, (n, 2, 2, ROPE_HALF)).reshape(n, DIFF_DH)
    return jnp.tile(cos, (1, LANES // DIFF_DH)), jnp.tile(sin, (1, LANES // DIFF_DH))


def _proj_weight(w_in):
    sizes = (GLA_QK, GLA_QK, GLA_WIDTH, GLA_WIDTH, 2 * GLA_GATE_RANK, DIFF_WIDTH, DIFF_WIDTH, DIFF_WIDTH)
    offs = [0]
    for s in sizes:
        offs.append(offs[-1] + s)
    part = lambda i: w_in[:, offs[i]:offs[i + 1]]
    pad = jnp.zeros((w_in.shape[0], DOWN_PAD - 2 * GLA_GATE_RANK), w_in.dtype)
    return jnp.concatenate([part(0), part(1), part(2), part(3), part(5), part(6), part(7), part(4), pad],
                           axis=1).astype(BF16)


def _tile_rows(n, pref):
    for t in pref:
        if n % t == 0:
            return t
    raise ValueError(f"no row tile for {n}")


def kernel(x, c, ctx, c_ctx, w_mod, b_mod, norm1_g, w_in, gla_gate_up, gla_gate_bias, gla_norm_g,
           diff_q_norm_g, diff_k_norm_g, diff_lambda_q, diff_lambda_k, diff_norm_g, w_out, norm2_g,
           w_ffn_in, w_ffn_out):
    assert w_mod.shape[0] == 1, "single-layer stack"
    b, n, d = x.shape
    n_ctx = ctx.shape[1]
    lam_init = 0.8 - 0.6 * math.exp(-0.3 * 0)

    cond = jnp.concatenate([c, c_ctx[None, :], jnp.zeros((-(b + 1) % SUBLANES, d), F32)], axis=0)
    mod = _mod_call(cond, w_mod[0], b_mod[0][None, :])
    sh1, sc1, gt1, sh2, sc2, gt2 = [mod[:, i * d:(i + 1) * d] for i in range(6)]
    lat = lambda m: m[0:b, None, :]
    ctx_rows = lambda m: jnp.broadcast_to(m[b:b + 1, None, :], (b, 1, d))

    w_p = _proj_weight(w_in[0])
    g1 = norm1_g[0][None, :]
    qg = jnp.tile(diff_q_norm_g[0], DIFF_WIDTH // DIFF_DH)[None, :]
    kg = jnp.tile(diff_k_norm_g[0], DIFF_WIDTH // DIFF_DH)[None, :]
    gid = jnp.arange(MXU_TILE) // DIFF_DH
    bd = (gid[:, None] == gid[None, :]).astype(BF16)
    cos_t, sin_t = _rope_tables(n)

    tm_c = _tile_rows(n_ctx, (256, 128, 64))
    tm_x = _tile_rows(n, (512, 256, 128, 64))
    pc = _proj_call(ctx, ctx_rows(sh1), ctx_rows(sc1), g1, w_p, qg, kg, cos_t[:n_ctx], sin_t[:n_ctx], bd,
                    rope=False, tm=tm_c)
    px = _proj_call(x, lat(sh1), lat(sc1), g1, w_p, qg, kg, cos_t, sin_t, bd, rope=True, tm=tm_x)
    gqk_c, gv_c, _, down_c, _, dk_c, vt_c = pc
    gqk, gv, gr, down, qt, dk, vt_lat = px

    gu = jnp.zeros((2, DOWN_PAD, GLA_QK), F32)
    for z in range(2):
        gu = gu.at[z, z * GLA_GATE_RANK:(z + 1) * GLA_GATE_RANK, :].set(gla_gate_up[0, z])
    bias = gla_gate_bias[0][:, None, :]
    zero_state = jnp.zeros((b, 2, GLA_WIDTH, GLA_QK), F32)
    _, _, s_ctx = _gla_call(gqk_c, gv_c, down_c, gu, bias, zero_state)
    o_f, o_b, _ = _gla_call(gqk, gv, down, gu, bias, s_ctx)

    tq = _tile_rows(n, (2 * ATTN_SUB_Q, ATTN_SUB_Q, 256, 128))
    assert n_ctx == tm_c, "context keys are one attention tile"
    vt_ctx = vt_c[:, 0]
    score_bound = 8.0 * jnp.max(jnp.abs(diff_q_norm_g[0])) * jnp.max(jnp.abs(diff_k_norm_g[0]))
    attn_args = (diff_lambda_q[0], diff_lambda_k[0], diff_norm_g[0][:, None], qt, dk_c, dk, vt_ctx, vt_lat)
    od = lax.cond(
        score_bound <= SCORE_BOUND,
        lambda *a: _attn_call(_attn_bounded_kernel, *a, lam_init=lam_init, tq=tq),
        lambda *a: _attn_call(_attn_online_kernel, *a, lam_init=lam_init, tq=tq),
        *attn_args)

    assert w_ffn_out.shape[1] % FFN_CHUNK == 0
    return _out_ffn_call(x, o_f, o_b, gr, od, lat(gt1), lat(sh2), lat(sc2), lat(gt2),
                         gla_norm_g[0][None, :], norm2_g[0][None, :], w_out[0].astype(BF16),
                         w_ffn_in[0].astype(BF16), w_ffn_out[0].astype(BF16), tm=tm_x)
```

```python
import functools
import math

import jax
import jax.numpy as jnp
from jax import lax
from jax.experimental import pallas as pl
from jax.experimental.pallas import tpu as pltpu

F32 = jnp.float32
BF16 = jnp.bfloat16

EPS = 1e-6
GRID_W = 64
GLA_HEADS = 4
GLA_DK = 64
GLA_DV = 128
GLA_QK = GLA_HEADS * GLA_DK
GLA_WIDTH = GLA_HEADS * GLA_DV
GLA_GATE_RANK = 16
GLA_GATE_NORM = 16.0
GLA_CHUNK = 64
DIFF_HEADS = 4
DIFF_DH = 64
DIFF_DV = 128
DIFF_WIDTH = DIFF_HEADS * DIFF_DV
ROPE_BASE = 10000.0
ROPE_AXIS_DIM = DIFF_DH // 2
ROPE_HALF = ROPE_AXIS_DIM // 2
LANES = 128
SUBLANES = 8
MXU_TILE = 256
DOWN_PAD = LANES
LOG2_E = math.log2(math.e)
SCORE_BOUND = 75.0

VMEM_LIMIT = 56 * 1024 * 1024
FFN_CHUNK = 256
ATTN_SUB_Q = 1024


def _dot(a, b):
    return jnp.dot(a, b, preferred_element_type=F32)


def _dot_nt(a, b):
    return lax.dot_general(a, b, (((1,), (1,)), ((), ())), preferred_element_type=F32)


def _dot_tn(a, b):
    return lax.dot_general(a, b, (((0,), (0,)), ((), ())), preferred_element_type=F32)


def _silu(x):
    return x / (1.0 + jnp.exp(-x))


def _split_bf16(x):
    hi = x.astype(BF16)
    return hi, (x - hi.astype(F32)).astype(BF16)


def _dot_split(a, b):
    ah, al = _split_bf16(a)
    bh, bl = _split_bf16(b)
    return _dot(ah, bh) + (_dot(al, bh) + _dot(ah, bl))


def _mod_kernel(c_ref, w_ref, b_ref, o_ref):
    o_ref[...] = _dot_split(_silu(c_ref[...]), w_ref[...]) + b_ref[...]


def _mod_call(cond, w_mod, b_mod):
    rows, d = cond.shape
    cols = w_mod.shape[1]
    bn = cols // 4
    return pl.pallas_call(
        _mod_kernel,
        grid=(cols // bn,),
        in_specs=[pl.BlockSpec((rows, d), lambda j: (0, 0)),
                  pl.BlockSpec((d, bn), lambda j: (0, j)),
                  pl.BlockSpec((1, bn), lambda j: (0, j))],
        out_specs=pl.BlockSpec((rows, bn), lambda j: (0, j)),
        out_shape=jax.ShapeDtypeStruct((rows, cols), F32),
        compiler_params=pltpu.CompilerParams(dimension_semantics=("arbitrary",),
                                             vmem_limit_bytes=VMEM_LIMIT),
        name="adaln_mod",
    )(cond, w_mod, b_mod)


_C_GQ, _C_GK, _C_GV, _C_GR, _C_DQ, _C_DK, _C_DV, _C_DOWN, _C_END = (
    0, 256, 512, 1024, 1536, 2048, 2560, 3072, 3072 + DOWN_PAD)


def _swap_halves(y):
    lane = lax.broadcasted_iota(jnp.int32, y.shape, 1)
    upper = (lane & ROPE_HALF) != 0
    return jnp.where(upper, pltpu.roll(y, ROPE_HALF, 1), pltpu.roll(y, LANES - ROPE_HALF, 1))


def _proj_kernel(x_ref, sh_ref, sc_ref, g1_ref, w_ref, qg_ref, kg_ref, cos_ref, sin_ref, bd_ref,
                 gqk_o, gv_o, gr_o, down_o, dq_o, dk_o, dv_o, *, rope):
    rs = slice(0, x_ref.shape[1])
    x = x_ref[0, rs, :]
    ms = jnp.mean(x * x, axis=-1, keepdims=True)
    h = x * lax.rsqrt(ms + EPS) * g1_ref[...]
    h = h * (1.0 + sc_ref[0]) + sh_ref[0]
    hb = h.astype(BF16)

    def mm(lo, hi):
        return _dot(hb, w_ref[:, lo:hi])

    gqk_o[0, rs, 0:GLA_QK] = (mm(_C_GQ, _C_GK) * (GLA_DK ** -0.5)).astype(BF16)
    gqk_o[0, rs, GLA_QK:2 * GLA_QK] = mm(_C_GK, _C_GV).astype(BF16)
    gv_o[0, rs, :] = mm(_C_GV, _C_GR).astype(BF16)
    gr_o[0, rs, :] = mm(_C_GR, _C_DQ).astype(BF16)
    dv = mm(_C_DV, _C_DOWN)
    for s in range(DIFF_WIDTH // LANES):
        dv_o[0, 0, s * LANES:(s + 1) * LANES, rs] = dv[:, s * LANES:(s + 1) * LANES].T.astype(BF16)
    down_o[0, rs, :] = mm(_C_DOWN, _C_END)

    def qk_norm(y, g_ref, scale, store):
        y2 = (y * y).astype(BF16)
        bw = bd_ref.shape[0]
        ss = jnp.concatenate([_dot(y2[:, lo:lo + bw], bd_ref[...]) for lo in range(0, DIFF_WIDTH, bw)], axis=1)
        yn = y * lax.rsqrt(ss * (1.0 / DIFF_DH) + EPS) * g_ref[...]
        for s in range(DIFF_WIDTH // LANES):
            slab = yn[:, s * LANES:(s + 1) * LANES]
            if rope:
                slab = slab * cos_ref[rs, :] + _swap_halves(slab) * sin_ref[rs, :]
            store(s, slab * scale)

    def store_q(s, slab):
        dq_o[0, s * LANES:(s + 1) * LANES, rs] = slab.T.astype(BF16)

    def store_k(s, slab):
        dk_o[0, rs, s * LANES:(s + 1) * LANES] = slab.astype(BF16)

    qk_norm(mm(_C_DQ, _C_DK), qg_ref, DIFF_DH ** -0.5 * LOG2_E, store_q)
    qk_norm(mm(_C_DK, _C_DV), kg_ref, 1.0, store_k)


def _proj_call(x, sh, sc, g1, w, qg, kg, cos_t, sin_t, bd, *, rope, tm):
    b, n, d = x.shape
    assert n % tm == 0
    tok = lambda width: pl.BlockSpec((1, tm, width), lambda i, t: (i, t, 0))
    const = lambda shape: pl.BlockSpec(shape, lambda i, t: tuple(0 for _ in shape))
    row = pl.BlockSpec((1, 1, d), lambda i, t: (i, 0, 0))
    outs = [(2 * GLA_QK, BF16), (GLA_WIDTH, BF16), (GLA_WIDTH, BF16), (DOWN_PAD, F32), (DIFF_WIDTH, BF16)]
    out_specs = [tok(wd) for wd, _ in outs]
    out_shape = [jax.ShapeDtypeStruct((b, n, wd), dt) for wd, dt in outs]
    out_specs.insert(4, pl.BlockSpec((1, DIFF_WIDTH, tm), lambda i, t: (i, 0, t)))
    out_shape.insert(4, jax.ShapeDtypeStruct((b, DIFF_WIDTH, n), BF16))
    out_specs.append(pl.BlockSpec((1, 1, DIFF_WIDTH, tm), lambda i, t: (i, t, 0, 0)))
    out_shape.append(jax.ShapeDtypeStruct((b, n // tm, DIFF_WIDTH, tm), BF16))
    return pl.pallas_call(
        functools.partial(_proj_kernel, rope=rope),
        grid=(b, n // tm),
        in_specs=[tok(d), row, row, const((1, d)), const(w.shape), const((1, DIFF_WIDTH)),
                  const((1, DIFF_WIDTH)),
                  pl.BlockSpec((tm, LANES), lambda i, t: (t, 0)),
                  pl.BlockSpec((tm, LANES), lambda i, t: (t, 0)),
                  const(bd.shape)],
        out_specs=out_specs,
        out_shape=out_shape,
        compiler_params=pltpu.CompilerParams(dimension_semantics=("arbitrary", "arbitrary"),
                                             vmem_limit_bytes=VMEM_LIMIT),
        name="in_proj_rope" if rope else "in_proj_ctx",
    )(x, sh, sc, g1, w, qg, kg, cos_t, sin_t, bd)


def _gla_prepare(qk_ref, dn_ref, gu, bias, tri, upper, z, prep):
    qin_s, qmid_s, kmid_s, kend_s, etot_s = prep
    nb, c = qk_ref.shape[0], qk_ref.shape[1]
    rows = nb * c
    down = dn_ref[...].reshape(rows, DOWN_PAD)
    dh, dl = _split_bf16(down)
    gh, gl = gu
    logits = _dot(dh, gh) + (_dot(dl, gh) + _dot(dh, gl)) + bias
    la = (jnp.minimum(logits, 0.0) - jnp.log(1.0 + jnp.exp(-jnp.abs(logits)))) * (1.0 / GLA_GATE_NORM)
    lh, ll = _split_bf16(la)
    cum_all = _dot(tri, lh) + _dot(tri, ll)

    for bi in range(nb):
        rs = slice(bi * c, (bi + 1) * c)
        cum = cum_all[rs]
        tot = cum[0:1] if upper else cum[c - 1:c]
        mid = cum[c // 2:c // 2 + 1]
        q = qk_ref[bi, :, 0:GLA_QK].astype(F32)
        k = qk_ref[bi, :, GLA_QK:2 * GLA_QK].astype(F32)
        qin_s[z, rs, :] = (q * jnp.exp(cum)).astype(BF16)
        qmid_s[z, rs, :] = q * jnp.exp(cum - mid)
        kmid_s[z, rs, :] = (k * jnp.exp(mid - cum)).astype(BF16)
        kend_s[z, rs, :] = (k * jnp.exp(tot - cum)).astype(BF16)
        etot_s[z, bi] = jnp.broadcast_to(jnp.exp(tot), etot_s.shape[2:])


def _gla_apply(prep, v_ref, st_ref, z, o_ref, head_lane, state_mask, tri_mask):
    qin_s, qmid_s, kmid_s, kend_s, etot_s = prep
    nb, c = v_ref.shape[0], v_ref.shape[1]
    q_mid = qmid_s[z]
    k_mid = kmid_s[z]
    v_all = v_ref[...].reshape(nb * c, GLA_WIDTH)
    outs = []
    for hd in range(GLA_HEADS):
        qh = jnp.where(head_lane == hd, q_mid, 0.0).astype(BF16)
        a = jnp.where(tri_mask, _dot_nt(qh, k_mid), 0.0).astype(BF16)
        outs.append(_dot(a, v_all[:, hd * GLA_DV:(hd + 1) * GLA_DV]))
    o_intra = jnp.concatenate(outs, axis=1)

    for bi in range(nb):
        rs = slice(bi * c, (bi + 1) * c)
        st = st_ref[bi, z]
        o_ref[bi] = o_intra[rs] + _dot_nt(qin_s[z, rs, :], st.astype(BF16))
        ut = _dot_tn(v_ref[bi], kend_s[z, rs, :])
        st_ref[bi, z] = etot_s[z, bi, 0:1, :] * st + jnp.where(state_mask, ut, 0.0)


def _gla_kernel(qk_f, dn_f, v_f, qk_b, dn_b, v_b, gu_ref, bias_ref, s0_ref,
                of_ref, ob_ref, sout_ref, st_ref, *prep):
    s = pl.program_id(0)
    nb, c = v_f.shape[0], v_f.shape[1]
    rows = nb * c

    @pl.when(s == 0)
    def _():
        st_ref[...] = s0_ref[...]
        for ref in prep[:-1]:
            ref[...] = jnp.zeros_like(ref)
        prep[-1][...] = jnp.ones_like(prep[-1])

    r = lax.broadcasted_iota(jnp.int32, (rows, rows), 0)
    cc = lax.broadcasted_iota(jnp.int32, (rows, rows), 1)
    same = (r // c) == (cc // c)
    lower = same & (r >= cc)
    upper = same & (r <= cc)
    head_lane = lax.broadcasted_iota(jnp.int32, (rows, GLA_QK), 1) // GLA_DK
    sr = lax.broadcasted_iota(jnp.int32, (GLA_WIDTH, GLA_QK), 0) // GLA_DV
    sl = lax.broadcasted_iota(jnp.int32, (GLA_WIDTH, GLA_QK), 1) // GLA_DK
    state_mask = sr == sl

    _gla_apply(prep, v_f, st_ref, 0, of_ref, head_lane, state_mask, lower)
    _gla_apply(prep, v_b, st_ref, 1, ob_ref, head_lane, state_mask, upper)
    _gla_prepare(qk_f, dn_f, _split_bf16(gu_ref[0]), bias_ref[0], lower.astype(BF16), False, 0, prep)
    _gla_prepare(qk_b, dn_b, _split_bf16(gu_ref[1]), bias_ref[1], upper.astype(BF16), True, 1, prep)

    @pl.when(s == pl.num_programs(0) - 1)
    def _():
        sout_ref[...] = st_ref[...]


def _gla_call(gqk, gv, down, gu, bias, s0):
    b, n, _ = gqk.shape
    c = GLA_CHUNK
    nc = n // c
    rows = b * c
    prep_i = lambda s: jnp.minimum(s, nc - 1)
    appl_i = lambda s: jnp.maximum(s - 1, 0)
    spec = lambda width, idx, mirror: pl.BlockSpec(
        (b, c, width), (lambda s: (0, nc - 1 - idx(s), 0)) if mirror else (lambda s: (0, idx(s), 0)))
    state = pl.BlockSpec((b, 2, GLA_WIDTH, GLA_QK), lambda s: (0, 0, 0, 0))
    return pl.pallas_call(
        _gla_kernel,
        grid=(nc + 1,),
        in_specs=[spec(2 * GLA_QK, prep_i, False), spec(DOWN_PAD, prep_i, False), spec(GLA_WIDTH, appl_i, False),
                  spec(2 * GLA_QK, prep_i, True), spec(DOWN_PAD, prep_i, True), spec(GLA_WIDTH, appl_i, True),
                  pl.BlockSpec(gu.shape, lambda s: (0, 0, 0)),
                  pl.BlockSpec(bias.shape, lambda s: (0, 0, 0)),
                  state],
        out_specs=[spec(GLA_WIDTH, appl_i, False), spec(GLA_WIDTH, appl_i, True), state],
        out_shape=[jax.ShapeDtypeStruct((b, n, GLA_WIDTH), F32),
                   jax.ShapeDtypeStruct((b, n, GLA_WIDTH), F32),
                   jax.ShapeDtypeStruct((b, 2, GLA_WIDTH, GLA_QK), F32)],
        scratch_shapes=[pltpu.VMEM((b, 2, GLA_WIDTH, GLA_QK), F32),
                        pltpu.VMEM((2, rows, GLA_QK), BF16), pltpu.VMEM((2, rows, GLA_QK), F32),
                        pltpu.VMEM((2, rows, GLA_QK), BF16), pltpu.VMEM((2, rows, GLA_QK), BF16),
                        pltpu.VMEM((2, b, SUBLANES, GLA_QK), F32)],
        compiler_params=pltpu.CompilerParams(dimension_semantics=("arbitrary",),
                                             vmem_limit_bytes=VMEM_LIMIT),
        name="gla_bidir",
    )(gqk, down, gv, gqk, down, gv, gu, bias, s0)


def _block_diag_queries(qt):
    row = lax.broadcasted_iota(jnp.int32, qt.shape, 0)
    zero = jnp.zeros_like(qt)
    return jnp.concatenate([jnp.where(row < DIFF_DH, qt, zero),
                            jnp.where(row >= DIFF_DH, qt, zero)], axis=1)


def _attn_finish(lq_ref, lk_ref, g_ref, o_ref, qs, acc, l, lam_init):
    tq = qs.stop - qs.start
    lqk = jnp.sum(lq_ref[...] * lk_ref[...], axis=1, keepdims=True)
    e = jnp.exp(lqk)
    lam = e[0:1] - e[1:2] + lam_init
    acc = acc * (1.0 / l)
    ot = acc[:, 0:tq] - lam * acc[:, tq:2 * tq]
    ms = jnp.mean(ot * ot, axis=0, keepdims=True)
    y = ot * lax.rsqrt(ms + EPS) * (g_ref[...] * (1.0 - lam_init))
    o_ref[0, qs, :] = y.T.astype(BF16)


def _query_subtiles(qt_ref):
    tq = min(qt_ref.shape[2], ATTN_SUB_Q)
    return [(i, slice(i * tq, (i + 1) * tq)) for i in range(qt_ref.shape[2] // tq)]


def _attn_online_kernel(lq_ref, lk_ref, g_ref, qt_ref, kc_ref, kl_ref, vtc_ref, vtl_ref, o_ref,
                        m_ref, l_ref, acc_ref, *, lam_init):
    tk = vtl_ref.shape[3]
    for _, qs in _query_subtiles(qt_ref):
        q_bd = _block_diag_queries(qt_ref[0, :, qs])

        s = _dot(kc_ref[0], q_bd)
        m0 = jnp.max(s, axis=0, keepdims=True)
        p = jnp.exp2(s - m0)
        m_ref[...] = m0
        l_ref[...] = jnp.sum(p, axis=0, keepdims=True)
        acc_ref[...] = _dot(vtc_ref[0], p.astype(BF16))

        def body(j, carry, q_bd=q_bd):
            s = _dot(kl_ref[0, pl.ds(pl.multiple_of(j * tk, tk), tk), :], q_bd)
            m_old = m_ref[...]
            m_new = jnp.maximum(m_old, jnp.max(s, axis=0, keepdims=True))
            alpha = jnp.exp2(m_old - m_new)
            p = jnp.exp2(s - m_new)
            m_ref[...] = m_new
            l_ref[...] = alpha * l_ref[...] + jnp.sum(p, axis=0, keepdims=True)
            acc_ref[...] = alpha * acc_ref[...] + _dot(vtl_ref[0, j], p.astype(BF16))
            return carry

        lax.fori_loop(0, vtl_ref.shape[1], body, 0)
        _attn_finish(lq_ref, lk_ref, g_ref, o_ref, qs, acc_ref[...], l_ref[...], lam_init)


def _attn_bounded_kernel(lq_ref, lk_ref, g_ref, qt_ref, kc_ref, kl_ref, vtc_ref, vtl_ref, o_ref,
                         s_ref, *, lam_init):
    tk = vtl_ref.shape[3]
    n_ctx = kc_ref.shape[1]
    tiles = [(n_ctx, lambda: kc_ref[0], lambda: vtc_ref[0])]
    for j in range(vtl_ref.shape[1]):
        tiles.append((tk, lambda j=j: kl_ref[0, j * tk:(j + 1) * tk, :], lambda j=j: vtl_ref[0, j]))

    for qi, qs in _query_subtiles(qt_ref):
        tq = qs.stop - qs.start
        q_bd = _block_diag_queries(qt_ref[0, :, qs])
        s_ref[qi % 2, 0:n_ctx, :] = _dot(tiles[0][1](), q_bd)
        l = jnp.zeros((1, 2 * tq), F32)
        acc = jnp.zeros((DIFF_DV, 2 * tq), F32)
        for t, (rows, _, vt) in enumerate(tiles):
            if t + 1 < len(tiles):
                nxt = tiles[t + 1]
                s_ref[(t + 1 + qi) % 2, 0:nxt[0], :] = _dot(nxt[1](), q_bd)
            p = jnp.exp2(s_ref[(t + qi) % 2, 0:rows, :])
            l = l + jnp.sum(p, axis=0, keepdims=True)
            acc = acc + _dot(vt(), p.astype(BF16))
        _attn_finish(lq_ref, lk_ref, g_ref, o_ref, qs, acc, l, lam_init)


def _attn_call(body, lq, lk, g_col, qt, k_ctx, k_lat, vt_ctx, vt_lat, *, lam_init, tq):
    b, _, n = qt.shape
    n_ctx = k_ctx.shape[1]
    nch, tk = vt_lat.shape[1], vt_lat.shape[3]
    sq = min(tq, ATTN_SUB_Q)
    const = lambda shape: pl.BlockSpec(shape, lambda bi, h, qi: tuple(0 for _ in shape))
    return pl.pallas_call(
        functools.partial(body, lam_init=lam_init),
        grid=(b, DIFF_HEADS, n // tq),
        in_specs=[const(lq.shape), const(lk.shape), const(g_col.shape),
                  pl.BlockSpec((1, DIFF_DV, tq), lambda bi, h, qi: (bi, h, qi)),
                  pl.BlockSpec((1, n_ctx, DIFF_DV), lambda bi, h, qi: (bi, 0, h)),
                  pl.BlockSpec((1, n, DIFF_DV), lambda bi, h, qi: (bi, 0, h)),
                  pl.BlockSpec((1, DIFF_DV, n_ctx), lambda bi, h, qi: (bi, h, 0)),
                  pl.BlockSpec((1, nch, DIFF_DV, tk), lambda bi, h, qi: (bi, 0, h, 0))],
        out_specs=pl.BlockSpec((1, tq, DIFF_DV), lambda bi, h, qi: (bi, qi, h)),
        out_shape=jax.ShapeDtypeStruct((b, n, DIFF_WIDTH), BF16),
        scratch_shapes=(
            [pltpu.VMEM((2, max(tk, n_ctx), 2 * sq), F32)] if body is _attn_bounded_kernel else
            [pltpu.VMEM((1, 2 * sq), F32), pltpu.VMEM((1, 2 * sq), F32), pltpu.VMEM((DIFF_DV, 2 * sq), F32)]),
        compiler_params=pltpu.CompilerParams(
            dimension_semantics=("arbitrary", "arbitrary", "arbitrary"),
            vmem_limit_bytes=VMEM_LIMIT),
        name=body.__name__.strip("_"),
    )(lq, lk, g_col, qt, k_ctx, k_lat, vt_ctx, vt_lat)


def _out_ffn_kernel(x_ref, of_ref, ob_ref, r_ref, od_ref, gt1_ref, sh2_ref, sc2_ref, gt2_ref,
                    gg_ref, g2_ref, wo_ref, wgu_ref, wd_ref, o_ref, acc_ref):
    og = of_ref[0] + ob_ref[0]
    r = r_ref[0].astype(F32)
    parts = []
    for hd in range(GLA_HEADS):
        sl = slice(hd * GLA_DV, (hd + 1) * GLA_DV)
        oh = og[:, sl]
        ms = jnp.mean(oh * oh, axis=-1, keepdims=True)
        parts.append((oh * lax.rsqrt(ms + EPS) * gg_ref[...] * _silu(r[:, sl])).astype(BF16))
    gla = jnp.concatenate(parts, axis=1)
    mix = _dot(gla, wo_ref[0:GLA_WIDTH, :]) + _dot(od_ref[0], wo_ref[GLA_WIDTH:, :])
    x1 = x_ref[0] + gt1_ref[0] * mix
    ms = jnp.mean(x1 * x1, axis=-1, keepdims=True)
    h = x1 * lax.rsqrt(ms + EPS) * g2_ref[...]
    hb = (h * (1.0 + sc2_ref[0]) + sh2_ref[0]).astype(BF16)

    fh = wd_ref.shape[0]
    acc_ref[...] = jnp.zeros_like(acc_ref)
    for lo in range(0, fh, FFN_CHUNK):
        gate = _dot(hb, wgu_ref[:, lo:lo + FFN_CHUNK])
        up = _dot(hb, wgu_ref[:, fh + lo:fh + lo + FFN_CHUNK])
        acc_ref[...] += _dot((_silu(gate) * up).astype(BF16), wd_ref[lo:lo + FFN_CHUNK, :])
    o_ref[0] = x1 + gt2_ref[0] * acc_ref[...]


def _out_ffn_call(x, o_f, o_b, gr, od, gt1, sh2, sc2, gt2, gg, g2, wo, wgu, wd, *, tm):
    b, n, d = x.shape
    tok = lambda width: pl.BlockSpec((1, tm, width), lambda i, t: (i, t, 0))
    row = pl.BlockSpec((1, 1, d), lambda i, t: (i, 0, 0))
    const = lambda shape: pl.BlockSpec(shape, lambda i, t: tuple(0 for _ in shape),
                                       pipeline_mode=pl.Buffered(1))
    return pl.pallas_call(
        _out_ffn_kernel,
        grid=(b, n // tm),
        in_specs=[tok(d), tok(GLA_WIDTH), tok(GLA_WIDTH), tok(GLA_WIDTH), tok(DIFF_WIDTH),
                  row, row, row, row, const(gg.shape), const(g2.shape),
                  const(wo.shape), const(wgu.shape), const(wd.shape)],
        out_specs=tok(d),
        out_shape=jax.ShapeDtypeStruct((b, n, d), F32),
        scratch_shapes=[pltpu.VMEM((tm, d), F32)],
        compiler_params=pltpu.CompilerParams(dimension_semantics=("arbitrary", "arbitrary"),
                                             vmem_limit_bytes=VMEM_LIMIT),
        name="out_proj_ffn",
    )(x, o_f, o_b, gr, od, gt1, sh2, sc2, gt2, gg, g2, wo, wgu, wd)


def _rope_tables(n):
    t = jnp.arange(n)
    pos = jnp.stack([(t // GRID_W).astype(F32), (t % GRID_W).astype(F32)], axis=1)
    inv_freq = ROPE_BASE ** (-jnp.arange(ROPE_HALF, dtype=F32) / ROPE_HALF)
    ang = pos[:, :, None] * inv_freq
    cos = jnp.cos(ang)[:, :, None, :]
    sin = jnp.sin(ang)[:, :, None, :] * jnp.array([-1.0, 1.0], F32)[None, None, :, None]
    cos = jnp.broadcast_to(cos, (n, 2, 2, ROPE_HALF)).reshape(n, DIFF_DH)
    sin = jnp.broadcast_to(sin, (n, 2, 2, ROPE_HALF)).reshape(n, DIFF_DH)
    return jnp.tile(cos, (1, LANES // DIFF_DH)), jnp.tile(sin, (1, LANES // DIFF_DH))


def _proj_weight(w_in):
    sizes = (GLA_QK, GLA_QK, GLA_WIDTH, GLA_WIDTH, 2 * GLA_GATE_RANK, DIFF_WIDTH, DIFF_WIDTH, DIFF_WIDTH)
    offs = [0]
    for s in sizes:
        offs.append(offs[-1] + s)
    part = lambda i: w_in[:, offs[i]:offs[i + 1]]
    pad = jnp.zeros((w_in.shape[0], DOWN_PAD - 2 * GLA_GATE_RANK), w_in.dtype)
    return jnp.concatenate([part(0), part(1), part(2), part(3), part(5), part(6), part(7), part(4), pad],
                           axis=1).astype(BF16)


def _tile_rows(n, pref):
    for t in pref:
        if n % t == 0:
            return t
    raise ValueError(f"no row tile for {n}")


def kernel(x, c, ctx, c_ctx, w_mod, b_mod, norm1_g, w_in, gla_gate_up, gla_gate_bias, gla_norm_g,
           diff_q_norm_g, diff_k_norm_g, diff_lambda_q, diff_lambda_k, diff_norm_g, w_out, norm2_g,
           w_ffn_in, w_ffn_out):
    assert w_mod.shape[0] == 1, "single-layer stack"
    b, n, d = x.shape
    n_ctx = ctx.shape[1]
    lam_init = 0.8 - 0.6 * math.exp(-0.3 * 0)

    cond = jnp.concatenate([c, c_ctx[None, :], jnp.zeros((-(b + 1) % SUBLANES, d), F32)], axis=0)
    mod = _mod_call(cond, w_mod[0], b_mod[0][None, :])
    sh1, sc1, gt1, sh2, sc2, gt2 = [mod[:, i * d:(i + 1) * d] for i in range(6)]
    lat = lambda m: m[0:b, None, :]
    ctx_rows = lambda m: jnp.broadcast_to(m[b:b + 1, None, :], (b, 1, d))

    w_p = _proj_weight(w_in[0])
    g1 = norm1_g[0][None, :]
    qg = jnp.tile(diff_q_norm_g[0], DIFF_WIDTH // DIFF_DH)[None, :]
    kg = jnp.tile(diff_k_norm_g[0], DIFF_WIDTH // DIFF_DH)[None, :]
    gid = jnp.arange(MXU_TILE) // DIFF_DH
    bd = (gid[:, None] == gid[None, :]).astype(BF16)
    cos_t, sin_t = _rope_tables(n)

    tm_c = _tile_rows(n_ctx, (256, 128, 64))
    tm_x = _tile_rows(n, (512, 256, 128, 64))
    pc = _proj_call(ctx, ctx_rows(sh1), ctx_rows(sc1), g1, w_p, qg, kg, cos_t[:n_ctx], sin_t[:n_ctx], bd,
                    rope=False, tm=tm_c)
    px = _proj_call(x, lat(sh1), lat(sc1), g1, w_p, qg, kg, cos_t, sin_t, bd, rope=True, tm=tm_x)
    gqk_c, gv_c, _, down_c, _, dk_c, vt_c = pc
    gqk, gv, gr, down, qt, dk, vt_lat = px

    gu = jnp.zeros((2, DOWN_PAD, GLA_QK), F32)
    for z in range(2):
        gu = gu.at[z, z * GLA_GATE_RANK:(z + 1) * GLA_GATE_RANK, :].set(gla_gate_up[0, z])
    bias = gla_gate_bias[0][:, None, :]
    zero_state = jnp.zeros((b, 2, GLA_WIDTH, GLA_QK), F32)
    _, _, s_ctx = _gla_call(gqk_c, gv_c, down_c, gu, bias, zero_state)
    o_f, o_b, _ = _gla_call(gqk, gv, down, gu, bias, s_ctx)

    tq = _tile_rows(n, (ATTN_SUB_Q, 512, 256, 128))
    assert n_ctx == tm_c, "context keys are one attention tile"
    vt_ctx = vt_c[:, 0]
    score_bound = 8.0 * jnp.max(jnp.abs(diff_q_norm_g[0])) * jnp.max(jnp.abs(diff_k_norm_g[0]))
    attn_args = (diff_lambda_q[0], diff_lambda_k[0], diff_norm_g[0][:, None], qt, dk_c, dk, vt_ctx, vt_lat)
    od = lax.cond(
        score_bound <= SCORE_BOUND,
        lambda *a: _attn_call(_attn_bounded_kernel, *a, lam_init=lam_init, tq=tq),
        lambda *a: _attn_call(_attn_online_kernel, *a, lam_init=lam_init, tq=tq),
        *attn_args)

    assert w_ffn_out.shape[1] % FFN_CHUNK == 0
    return _out_ffn_call(x, o_f, o_b, gr, od, lat(gt1), lat(sh2), lat(sc2), lat(gt2),
                         gla_norm_g[0][None, :], norm2_g[0][None, :], w_out[0].astype(BF16),
                         w_ffn_in[0].astype(BF16), w_ffn_out[0].astype(BF16), tm=tm_x)
```

```python
import functools
import math

import jax
import jax.numpy as jnp
from jax import lax
from jax.experimental import pallas as pl
from jax.experimental.pallas import tpu as pltpu

F32 = jnp.float32
BF16 = jnp.bfloat16

EPS = 1e-6
GRID_W = 64
GLA_HEADS = 4
GLA_DK = 64
GLA_DV = 128
GLA_QK = GLA_HEADS * GLA_DK
GLA_WIDTH = GLA_HEADS * GLA_DV
GLA_GATE_RANK = 16
GLA_GATE_NORM = 16.0
GLA_CHUNK = 64
DIFF_HEADS = 4
DIFF_DH = 64
DIFF_DV = 128
DIFF_WIDTH = DIFF_HEADS * DIFF_DV
ROPE_BASE = 10000.0
ROPE_AXIS_DIM = DIFF_DH // 2
ROPE_HALF = ROPE_AXIS_DIM // 2
LANES = 128
SUBLANES = 8
MXU_TILE = 256
DOWN_PAD = LANES
LOG2_E = math.log2(math.e)
SCORE_BOUND = 75.0

VMEM_LIMIT = 56 * 1024 * 1024
FFN_CHUNK = 256
ATTN_SUB_Q = 2048


def _dot(a, b):
    return jnp.dot(a, b, preferred_element_type=F32)


def _dot_nt(a, b):
    return lax.dot_general(a, b, (((1,), (1,)), ((), ())), preferred_element_type=F32)


def _dot_tn(a, b):
    return lax.dot_general(a, b, (((0,), (0,)), ((), ())), preferred_element_type=F32)


def _silu(x):
    return x / (1.0 + jnp.exp(-x))


def _split_bf16(x):
    hi = x.astype(BF16)
    return hi, (x - hi.astype(F32)).astype(BF16)


def _dot_split(a, b):
    ah, al = _split_bf16(a)
    bh, bl = _split_bf16(b)
    return _dot(ah, bh) + (_dot(al, bh) + _dot(ah, bl))


def _mod_kernel(c_ref, w_ref, b_ref, o_ref):
    o_ref[...] = _dot_split(_silu(c_ref[...]), w_ref[...]) + b_ref[...]


def _mod_call(cond, w_mod, b_mod):
    rows, d = cond.shape
    cols = w_mod.shape[1]
    bn = cols // 4
    return pl.pallas_call(
        _mod_kernel,
        grid=(cols // bn,),
        in_specs=[pl.BlockSpec((rows, d), lambda j: (0, 0)),
                  pl.BlockSpec((d, bn), lambda j: (0, j)),
                  pl.BlockSpec((1, bn), lambda j: (0, j))],
        out_specs=pl.BlockSpec((rows, bn), lambda j: (0, j)),
        out_shape=jax.ShapeDtypeStruct((rows, cols), F32),
        compiler_params=pltpu.CompilerParams(dimension_semantics=("arbitrary",),
                                             vmem_limit_bytes=VMEM_LIMIT),
        name="adaln_mod",
    )(cond, w_mod, b_mod)


_C_GQ, _C_GK, _C_GV, _C_GR, _C_DQ, _C_DK, _C_DV, _C_DOWN, _C_END = (
    0, 256, 512, 1024, 1536, 2048, 2560, 3072, 3072 + DOWN_PAD)


def _swap_halves(y):
    lane = lax.broadcasted_iota(jnp.int32, y.shape, 1)
    upper = (lane & ROPE_HALF) != 0
    return jnp.where(upper, pltpu.roll(y, ROPE_HALF, 1), pltpu.roll(y, LANES - ROPE_HALF, 1))


def _proj_kernel(x_ref, sh_ref, sc_ref, g1_ref, w_ref, qg_ref, kg_ref, cos_ref, sin_ref, bd_ref,
                 gqk_o, gv_o, gr_o, down_o, dq_o, dk_o, dv_o, *, rope):
    rs = slice(0, x_ref.shape[1])
    x = x_ref[0, rs, :]
    ms = jnp.mean(x * x, axis=-1, keepdims=True)
    h = x * lax.rsqrt(ms + EPS) * g1_ref[...]
    h = h * (1.0 + sc_ref[0]) + sh_ref[0]
    hb = h.astype(BF16)

    def mm(lo, hi):
        return _dot(hb, w_ref[:, lo:hi])

    gqk_o[0, rs, 0:GLA_QK] = (mm(_C_GQ, _C_GK) * (GLA_DK ** -0.5)).astype(BF16)
    gqk_o[0, rs, GLA_QK:2 * GLA_QK] = mm(_C_GK, _C_GV).astype(BF16)
    gv_o[0, rs, :] = mm(_C_GV, _C_GR).astype(BF16)
    gr_o[0, rs, :] = mm(_C_GR, _C_DQ).astype(BF16)
    dv = mm(_C_DV, _C_DOWN)
    for s in range(DIFF_WIDTH // LANES):
        dv_o[0, 0, s * LANES:(s + 1) * LANES, rs] = dv[:, s * LANES:(s + 1) * LANES].T.astype(BF16)
    down_o[0, rs, :] = mm(_C_DOWN, _C_END)

    def qk_norm(y, g_ref, scale, store):
        y2 = (y * y).astype(BF16)
        bw = bd_ref.shape[0]
        ss = jnp.concatenate([_dot(y2[:, lo:lo + bw], bd_ref[...]) for lo in range(0, DIFF_WIDTH, bw)], axis=1)
        yn = y * lax.rsqrt(ss * (1.0 / DIFF_DH) + EPS) * g_ref[...]
        for s in range(DIFF_WIDTH // LANES):
            slab = yn[:, s * LANES:(s + 1) * LANES]
            if rope:
                slab = slab * cos_ref[rs, :] + _swap_halves(slab) * sin_ref[rs, :]
            store(s, slab * scale)

    def store_q(s, slab):
        dq_o[0, s * LANES:(s + 1) * LANES, rs] = slab.T.astype(BF16)

    def store_k(s, slab):
        dk_o[0, rs, s * LANES:(s + 1) * LANES] = slab.astype(BF16)

    qk_norm(mm(_C_DQ, _C_DK), qg_ref, DIFF_DH ** -0.5 * LOG2_E, store_q)
    qk_norm(mm(_C_DK, _C_DV), kg_ref, 1.0, store_k)


def _proj_call(x, sh, sc, g1, w, qg, kg, cos_t, sin_t, bd, *, rope, tm):
    b, n, d = x.shape
    assert n % tm == 0
    tok = lambda width: pl.BlockSpec((1, tm, width), lambda i, t: (i, t, 0))
    const = lambda shape: pl.BlockSpec(shape, lambda i, t: tuple(0 for _ in shape))
    row = pl.BlockSpec((1, 1, d), lambda i, t: (i, 0, 0))
    outs = [(2 * GLA_QK, BF16), (GLA_WIDTH, BF16), (GLA_WIDTH, BF16), (DOWN_PAD, F32), (DIFF_WIDTH, BF16)]
    out_specs = [tok(wd) for wd, _ in outs]
    out_shape = [jax.ShapeDtypeStruct((b, n, wd), dt) for wd, dt in outs]
    out_specs.insert(4, pl.BlockSpec((1, DIFF_WIDTH, tm), lambda i, t: (i, 0, t)))
    out_shape.insert(4, jax.ShapeDtypeStruct((b, DIFF_WIDTH, n), BF16))
    out_specs.append(pl.BlockSpec((1, 1, DIFF_WIDTH, tm), lambda i, t: (i, t, 0, 0)))
    out_shape.append(jax.ShapeDtypeStruct((b, n // tm, DIFF_WIDTH, tm), BF16))
    return pl.pallas_call(
        functools.partial(_proj_kernel, rope=rope),
        grid=(b, n // tm),
        in_specs=[tok(d), row, row, const((1, d)), const(w.shape), const((1, DIFF_WIDTH)),
                  const((1, DIFF_WIDTH)),
                  pl.BlockSpec((tm, LANES), lambda i, t: (t, 0)),
                  pl.BlockSpec((tm, LANES), lambda i, t: (t, 0)),
                  const(bd.shape)],
        out_specs=out_specs,
        out_shape=out_shape,
        compiler_params=pltpu.CompilerParams(dimension_semantics=("arbitrary", "arbitrary"),
                                             vmem_limit_bytes=VMEM_LIMIT),
        name="in_proj_rope" if rope else "in_proj_ctx",
    )(x, sh, sc, g1, w, qg, kg, cos_t, sin_t, bd)


def _gla_prepare(qk_ref, dn_ref, gu, bias, tri, upper, z, prep):
    qin_s, qmid_s, kmid_s, kend_s, etot_s = prep
    nb, c = qk_ref.shape[0], qk_ref.shape[1]
    rows = nb * c
    down = dn_ref[...].reshape(rows, DOWN_PAD)
    dh, dl = _split_bf16(down)
    gh, gl = gu
    logits = _dot(dh, gh) + (_dot(dl, gh) + _dot(dh, gl)) + bias
    la = (jnp.minimum(logits, 0.0) - jnp.log(1.0 + jnp.exp(-jnp.abs(logits)))) * (1.0 / GLA_GATE_NORM)
    lh, ll = _split_bf16(la)
    cum_all = _dot(tri, lh) + _dot(tri, ll)

    for bi in range(nb):
        rs = slice(bi * c, (bi + 1) * c)
        cum = cum_all[rs]
        tot = cum[0:1] if upper else cum[c - 1:c]
        mid = cum[c // 2:c // 2 + 1]
        q = qk_ref[bi, :, 0:GLA_QK].astype(F32)
        k = qk_ref[bi, :, GLA_QK:2 * GLA_QK].astype(F32)
        qin_s[z, rs, :] = (q * jnp.exp(cum)).astype(BF16)
        qmid_s[z, rs, :] = q * jnp.exp(cum - mid)
        kmid_s[z, rs, :] = (k * jnp.exp(mid - cum)).astype(BF16)
        kend_s[z, rs, :] = (k * jnp.exp(tot - cum)).astype(BF16)
        etot_s[z, bi] = jnp.broadcast_to(jnp.exp(tot), etot_s.shape[2:])


def _gla_apply(prep, v_ref, st_ref, z, o_ref, head_lane, state_mask, tri_mask):
    qin_s, qmid_s, kmid_s, kend_s, etot_s = prep
    nb, c = v_ref.shape[0], v_ref.shape[1]
    q_mid = qmid_s[z]
    k_mid = kmid_s[z]
    v_all = v_ref[...].reshape(nb * c, GLA_WIDTH)
    outs = []
    for hd in range(GLA_HEADS):
        qh = jnp.where(head_lane == hd, q_mid, 0.0).astype(BF16)
        a = jnp.where(tri_mask, _dot_nt(qh, k_mid), 0.0).astype(BF16)
        outs.append(_dot(a, v_all[:, hd * GLA_DV:(hd + 1) * GLA_DV]))
    o_intra = jnp.concatenate(outs, axis=1)

    for bi in range(nb):
        rs = slice(bi * c, (bi + 1) * c)
        st = st_ref[bi, z]
        o_ref[bi] = o_intra[rs] + _dot_nt(qin_s[z, rs, :], st.astype(BF16))
        ut = _dot_tn(v_ref[bi], kend_s[z, rs, :])
        st_ref[bi, z] = etot_s[z, bi, 0:1, :] * st + jnp.where(state_mask, ut, 0.0)


def _gla_kernel(qk_f, dn_f, v_f, qk_b, dn_b, v_b, gu_ref, bias_ref, s0_ref,
                of_ref, ob_ref, sout_ref, st_ref, *prep):
    s = pl.program_id(0)
    nb, c = v_f.shape[0], v_f.shape[1]
    rows = nb * c

    @pl.when(s == 0)
    def _():
        st_ref[...] = s0_ref[...]
        for ref in prep[:-1]:
            ref[...] = jnp.zeros_like(ref)
        prep[-1][...] = jnp.ones_like(prep[-1])

    r = lax.broadcasted_iota(jnp.int32, (rows, rows), 0)
    cc = lax.broadcasted_iota(jnp.int32, (rows, rows), 1)
    same = (r // c) == (cc // c)
    lower = same & (r >= cc)
    upper = same & (r <= cc)
    head_lane = lax.broadcasted_iota(jnp.int32, (rows, GLA_QK), 1) // GLA_DK
    sr = lax.broadcasted_iota(jnp.int32, (GLA_WIDTH, GLA_QK), 0) // GLA_DV
    sl = lax.broadcasted_iota(jnp.int32, (GLA_WIDTH, GLA_QK), 1) // GLA_DK
    state_mask = sr == sl

    _gla_apply(prep, v_f, st_ref, 0, of_ref, head_lane, state_mask, lower)
    _gla_apply(prep, v_b, st_ref, 1, ob_ref, head_lane, state_mask, upper)
    _gla_prepare(qk_f, dn_f, _split_bf16(gu_ref[0]), bias_ref[0], lower.astype(BF16), False, 0, prep)
    _gla_prepare(qk_b, dn_b, _split_bf16(gu_ref[1]), bias_ref[1], upper.astype(BF16), True, 1, prep)

    @pl.when(s == pl.num_programs(0) - 1)
    def _():
        sout_ref[...] = st_ref[...]


def _gla_call(gqk, gv, down, gu, bias, s0):
    b, n, _ = gqk.shape
    c = GLA_CHUNK
    nc = n // c
    rows = b * c
    prep_i = lambda s: jnp.minimum(s, nc - 1)
    appl_i = lambda s: jnp.maximum(s - 1, 0)
    spec = lambda width, idx, mirror: pl.BlockSpec(
        (b, c, width), (lambda s: (0, nc - 1 - idx(s), 0)) if mirror else (lambda s: (0, idx(s), 0)))
    state = pl.BlockSpec((b, 2, GLA_WIDTH, GLA_QK), lambda s: (0, 0, 0, 0))
    return pl.pallas_call(
        _gla_kernel,
        grid=(nc + 1,),
        in_specs=[spec(2 * GLA_QK, prep_i, False), spec(DOWN_PAD, prep_i, False), spec(GLA_WIDTH, appl_i, False),
                  spec(2 * GLA_QK, prep_i, True), spec(DOWN_PAD, prep_i, True), spec(GLA_WIDTH, appl_i, True),
                  pl.BlockSpec(gu.shape, lambda s: (0, 0, 0)),
                  pl.BlockSpec(bias.shape, lambda s: (0, 0, 0)),
                  state],
        out_specs=[spec(GLA_WIDTH, appl_i, False), spec(GLA_WIDTH, appl_i, True), state],
        out_shape=[jax.ShapeDtypeStruct((b, n, GLA_WIDTH), F32),
                   jax.ShapeDtypeStruct((b, n, GLA_WIDTH), F32),
                   jax.ShapeDtypeStruct((b, 2, GLA_WIDTH, GLA_QK), F32)],
        scratch_shapes=[pltpu.VMEM((b, 2, GLA_WIDTH, GLA_QK), F32),
                        pltpu.VMEM((2, rows, GLA_QK), BF16), pltpu.VMEM((2, rows, GLA_QK), F32),
                        pltpu.VMEM((2, rows, GLA_QK), BF16), pltpu.VMEM((2, rows, GLA_QK), BF16),
                        pltpu.VMEM((2, b, SUBLANES, GLA_QK), F32)],
        compiler_params=pltpu.CompilerParams(dimension_semantics=("arbitrary",),
                                             vmem_limit_bytes=VMEM_LIMIT),
        name="gla_bidir",
    )(gqk, down, gv, gqk, down, gv, gu, bias, s0)


def _block_diag_queries(qt):
    row = lax.broadcasted_iota(jnp.int32, qt.shape, 0)
    zero = jnp.zeros_like(qt)
    return jnp.concatenate([jnp.where(row < DIFF_DH, qt, zero),
                            jnp.where(row >= DIFF_DH, qt, zero)], axis=1)


def _attn_finish(lq_ref, lk_ref, g_ref, o_ref, qs, acc, l, lam_init):
    tq = qs.stop - qs.start
    lqk = jnp.sum(lq_ref[...] * lk_ref[...], axis=1, keepdims=True)
    e = jnp.exp(lqk)
    lam = e[0:1] - e[1:2] + lam_init
    acc = acc * (1.0 / l)
    ot = acc[:, 0:tq] - lam * acc[:, tq:2 * tq]
    ms = jnp.mean(ot * ot, axis=0, keepdims=True)
    y = ot * lax.rsqrt(ms + EPS) * (g_ref[...] * (1.0 - lam_init))
    o_ref[0, qs, :] = y.T.astype(BF16)


def _query_subtiles(qt_ref):
    tq = min(qt_ref.shape[2], ATTN_SUB_Q)
    return [(i, slice(i * tq, (i + 1) * tq)) for i in range(qt_ref.shape[2] // tq)]


def _attn_online_kernel(lq_ref, lk_ref, g_ref, qt_ref, kc_ref, kl_ref, vtc_ref, vtl_ref, o_ref,
                        m_ref, l_ref, acc_ref, *, lam_init):
    tk = vtl_ref.shape[3]
    for _, qs in _query_subtiles(qt_ref):
        q_bd = _block_diag_queries(qt_ref[0, :, qs])

        s = _dot(kc_ref[0], q_bd)
        m0 = jnp.max(s, axis=0, keepdims=True)
        p = jnp.exp2(s - m0)
        m_ref[...] = m0
        l_ref[...] = jnp.sum(p, axis=0, keepdims=True)
        acc_ref[...] = _dot(vtc_ref[0], p.astype(BF16))

        def body(j, carry, q_bd=q_bd):
            s = _dot(kl_ref[0, pl.ds(pl.multiple_of(j * tk, tk), tk), :], q_bd)
            m_old = m_ref[...]
            m_new = jnp.maximum(m_old, jnp.max(s, axis=0, keepdims=True))
            alpha = jnp.exp2(m_old - m_new)
            p = jnp.exp2(s - m_new)
            m_ref[...] = m_new
            l_ref[...] = alpha * l_ref[...] + jnp.sum(p, axis=0, keepdims=True)
            acc_ref[...] = alpha * acc_ref[...] + _dot(vtl_ref[0, j], p.astype(BF16))
            return carry

        lax.fori_loop(0, vtl_ref.shape[1], body, 0)
        _attn_finish(lq_ref, lk_ref, g_ref, o_ref, qs, acc_ref[...], l_ref[...], lam_init)


def _attn_bounded_kernel(lq_ref, lk_ref, g_ref, qt_ref, kc_ref, kl_ref, vtc_ref, vtl_ref, o_ref,
                         s_ref, *, lam_init):
    tk = vtl_ref.shape[3]
    n_ctx = kc_ref.shape[1]
    tiles = [(n_ctx, lambda: kc_ref[0], lambda: vtc_ref[0])]
    for j in range(vtl_ref.shape[1]):
        tiles.append((tk, lambda j=j: kl_ref[0, j * tk:(j + 1) * tk, :], lambda j=j: vtl_ref[0, j]))

    for qi, qs in _query_subtiles(qt_ref):
        tq = qs.stop - qs.start
        q_bd = _block_diag_queries(qt_ref[0, :, qs])
        s_ref[qi % 2, 0:n_ctx, :] = _dot(tiles[0][1](), q_bd)
        l = jnp.zeros((1, 2 * tq), F32)
        acc = jnp.zeros((DIFF_DV, 2 * tq), F32)
        for t, (rows, _, vt) in enumerate(tiles):
            if t + 1 < len(tiles):
                nxt = tiles[t + 1]
                s_ref[(t + 1 + qi) % 2, 0:nxt[0], :] = _dot(nxt[1](), q_bd)
            p = jnp.exp2(s_ref[(t + qi) % 2, 0:rows, :])
            l = l + jnp.sum(p, axis=0, keepdims=True)
            acc = acc + _dot(vt(), p.astype(BF16))
        _attn_finish(lq_ref, lk_ref, g_ref, o_ref, qs, acc, l, lam_init)


def _attn_call(body, lq, lk, g_col, qt, k_ctx, k_lat, vt_ctx, vt_lat, *, lam_init, tq):
    b, _, n = qt.shape
    n_ctx = k_ctx.shape[1]
    nch, tk = vt_lat.shape[1], vt_lat.shape[3]
    sq = min(tq, ATTN_SUB_Q)
    const = lambda shape: pl.BlockSpec(shape, lambda bi, h, qi: tuple(0 for _ in shape))
    return pl.pallas_call(
        functools.partial(body, lam_init=lam_init),
        grid=(b, DIFF_HEADS, n // tq),
        in_specs=[const(lq.shape), const(lk.shape), const(g_col.shape),
                  pl.BlockSpec((1, DIFF_DV, tq), lambda bi, h, qi: (bi, h, qi)),
                  pl.BlockSpec((1, n_ctx, DIFF_DV), lambda bi, h, qi: (bi, 0, h)),
                  pl.BlockSpec((1, n, DIFF_DV), lambda bi, h, qi: (bi, 0, h)),
                  pl.BlockSpec((1, DIFF_DV, n_ctx), lambda bi, h, qi: (bi, h, 0)),
                  pl.BlockSpec((1, nch, DIFF_DV, tk), lambda bi, h, qi: (bi, 0, h, 0))],
        out_specs=pl.BlockSpec((1, tq, DIFF_DV), lambda bi, h, qi: (bi, qi, h)),
        out_shape=jax.ShapeDtypeStruct((b, n, DIFF_WIDTH), BF16),
        scratch_shapes=(
            [pltpu.VMEM((2, max(tk, n_ctx), 2 * sq), F32)] if body is _attn_bounded_kernel else
            [pltpu.VMEM((1, 2 * sq), F32), pltpu.VMEM((1, 2 * sq), F32), pltpu.VMEM((DIFF_DV, 2 * sq), F32)]),
        compiler_params=pltpu.CompilerParams(
            dimension_semantics=("arbitrary", "arbitrary", "arbitrary"),
            vmem_limit_bytes=VMEM_LIMIT),
        name=body.__name__.strip("_"),
    )(lq, lk, g_col, qt, k_ctx, k_lat, vt_ctx, vt_lat)


def _out_ffn_kernel(x_ref, of_ref, ob_ref, r_ref, od_ref, gt1_ref, sh2_ref, sc2_ref, gt2_ref,
                    gg_ref, g2_ref, wo_ref, wgu_ref, wd_ref, o_ref, acc_ref):
    og = of_ref[0] + ob_ref[0]
    r = r_ref[0].astype(F32)
    parts = []
    for hd in range(GLA_HEADS):
        sl = slice(hd * GLA_DV, (hd + 1) * GLA_DV)
        oh = og[:, sl]
        ms = jnp.mean(oh * oh, axis=-1, keepdims=True)
        parts.append((oh * lax.rsqrt(ms + EPS) * gg_ref[...] * _silu(r[:, sl])).astype(BF16))
    gla = jnp.concatenate(parts, axis=1)
    mix = _dot(gla, wo_ref[0:GLA_WIDTH, :]) + _dot(od_ref[0], wo_ref[GLA_WIDTH:, :])
    x1 = x_ref[0] + gt1_ref[0] * mix
    ms = jnp.mean(x1 * x1, axis=-1, keepdims=True)
    h = x1 * lax.rsqrt(ms + EPS) * g2_ref[...]
    hb = (h * (1.0 + sc2_ref[0]) + sh2_ref[0]).astype(BF16)

    fh = wd_ref.shape[0]
    acc_ref[...] = jnp.zeros_like(acc_ref)
    for lo in range(0, fh, FFN_CHUNK):
        gate = _dot(hb, wgu_ref[:, lo:lo + FFN_CHUNK])
        up = _dot(hb, wgu_ref[:, fh + lo:fh + lo + FFN_CHUNK])
        acc_ref[...] += _dot((_silu(gate) * up).astype(BF16), wd_ref[lo:lo + FFN_CHUNK, :])
    o_ref[0] = x1 + gt2_ref[0] * acc_ref[...]


def _out_ffn_call(x, o_f, o_b, gr, od, gt1, sh2, sc2, gt2, gg, g2, wo, wgu, wd, *, tm):
    b, n, d = x.shape
    tok = lambda width: pl.BlockSpec((1, tm, width), lambda i, t: (i, t, 0))
    row = pl.BlockSpec((1, 1, d), lambda i, t: (i, 0, 0))
    const = lambda shape: pl.BlockSpec(shape, lambda i, t: tuple(0 for _ in shape),
                                       pipeline_mode=pl.Buffered(1))
    return pl.pallas_call(
        _out_ffn_kernel,
        grid=(b, n // tm),
        in_specs=[tok(d), tok(GLA_WIDTH), tok(GLA_WIDTH), tok(GLA_WIDTH), tok(DIFF_WIDTH),
                  row, row, row, row, const(gg.shape), const(g2.shape),
                  const(wo.shape), const(wgu.shape), const(wd.shape)],
        out_specs=tok(d),
        out_shape=jax.ShapeDtypeStruct((b, n, d), F32),
        scratch_shapes=[pltpu.VMEM((tm, d), F32)],
        compiler_params=pltpu.CompilerParams(dimension_semantics=("arbitrary", "arbitrary"),
                                             vmem_limit_bytes=VMEM_LIMIT),
        name="out_proj_ffn",
    )(x, o_f, o_b, gr, od, gt1, sh2, sc2, gt2, gg, g2, wo, wgu, wd)


def _rope_tables(n):
    t = jnp.arange(n)
    pos = jnp.stack([(t // GRID_W).astype(F32), (t % GRID_W).astype(F32)], axis=1)
    inv_freq = ROPE_BASE ** (-jnp.arange(ROPE_HALF, dtype=F32) / ROPE_HALF)
    ang = pos[:, :, None] * inv_freq
    cos = jnp.cos(ang)[:, :, None, :]
    sin = jnp.sin(ang)[:, :, None, :] * jnp.array([-1.0, 1.0], F32)[None, None, :, None]
    cos = jnp.broadcast_to(cos, (n, 2, 2, ROPE_HALF)).reshape(n, DIFF_DH)
    sin = jnp.broadcast_to(sin, (n, 2, 2, ROPE_HALF)).reshape(n, DIFF_DH)
    return jnp.tile(cos, (1, LANES // DIFF_DH)), jnp.tile(sin, (1, LANES // DIFF_DH))


def _proj_weight(w_in):
    sizes = (GLA_QK, GLA_QK, GLA_WIDTH, GLA_WIDTH, 2 * GLA_GATE_RANK, DIFF_WIDTH, DIFF_WIDTH, DIFF_WIDTH)
    offs = [0]
    for s in sizes:
        offs.append(offs[-1] + s)
    part = lambda i: w_in[:, offs[i]:offs[i + 1]]
    pad = jnp.zeros((w_in.shape[0], DOWN_PAD - 2 * GLA_GATE_RANK), w_in.dtype)
    return jnp.concatenate([part(0), part(1), part(2), part(3), part(5), part(6), part(7), part(4), pad],
                           axis=1).astype(BF16)


def _tile_rows(n, pref):
    for t in pref:
        if n % t == 0:
            return t
    raise ValueError(f"no row tile for {n}")


def kernel(x, c, ctx, c_ctx, w_mod, b_mod, norm1_g, w_in, gla_gate_up, gla_gate_bias, gla_norm_g,
           diff_q_norm_g, diff_k_norm_g, diff_lambda_q, diff_lambda_k, diff_norm_g, w_out, norm2_g,
           w_ffn_in, w_ffn_out):
    assert w_mod.shape[0] == 1, "single-layer stack"
    b, n, d = x.shape
    n_ctx = ctx.shape[1]
    lam_init = 0.8 - 0.6 * math.exp(-0.3 * 0)

    cond = jnp.concatenate([c, c_ctx[None, :], jnp.zeros((-(b + 1) % SUBLANES, d), F32)], axis=0)
    mod = _mod_call(cond, w_mod[0], b_mod[0][None, :])
    sh1, sc1, gt1, sh2, sc2, gt2 = [mod[:, i * d:(i + 1) * d] for i in range(6)]
    lat = lambda m: m[0:b, None, :]
    ctx_rows = lambda m: jnp.broadcast_to(m[b:b + 1, None, :], (b, 1, d))

    w_p = _proj_weight(w_in[0])
    g1 = norm1_g[0][None, :]
    qg = jnp.tile(diff_q_norm_g[0], DIFF_WIDTH // DIFF_DH)[None, :]
    kg = jnp.tile(diff_k_norm_g[0], DIFF_WIDTH // DIFF_DH)[None, :]
    gid = jnp.arange(MXU_TILE) // DIFF_DH
    bd = (gid[:, None] == gid[None, :]).astype(BF16)
    cos_t, sin_t = _rope_tables(n)

    tm_c = _tile_rows(n_ctx, (256, 128, 64))
    tm_x = _tile_rows(n, (512, 256, 128, 64))
    pc = _proj_call(ctx, ctx_rows(sh1), ctx_rows(sc1), g1, w_p, qg, kg, cos_t[:n_ctx], sin_t[:n_ctx], bd,
                    rope=False, tm=tm_c)
    px = _proj_call(x, lat(sh1), lat(sc1), g1, w_p, qg, kg, cos_t, sin_t, bd, rope=True, tm=tm_x)
    gqk_c, gv_c, _, down_c, _, dk_c, vt_c = pc
    gqk, gv, gr, down, qt, dk, vt_lat = px

    gu = jnp.zeros((2, DOWN_PAD, GLA_QK), F32)
    for z in range(2):
        gu = gu.at[z, z * GLA_GATE_RANK:(z + 1) * GLA_GATE_RANK, :].set(gla_gate_up[0, z])
    bias = gla_gate_bias[0][:, None, :]
    zero_state = jnp.zeros((b, 2, GLA_WIDTH, GLA_QK), F32)
    _, _, s_ctx = _gla_call(gqk_c, gv_c, down_c, gu, bias, zero_state)
    o_f, o_b, _ = _gla_call(gqk, gv, down, gu, bias, s_ctx)

    tq = _tile_rows(n, (ATTN_SUB_Q, 512, 256, 128))
    assert n_ctx == tm_c, "context keys are one attention tile"
    vt_ctx = vt_c[:, 0]
    score_bound = 8.0 * jnp.max(jnp.abs(diff_q_norm_g[0])) * jnp.max(jnp.abs(diff_k_norm_g[0]))
    attn_args = (diff_lambda_q[0], diff_lambda_k[0], diff_norm_g[0][:, None], qt, dk_c, dk, vt_ctx, vt_lat)
    od = lax.cond(
        score_bound <= SCORE_BOUND,
        lambda *a: _attn_call(_attn_bounded_kernel, *a, lam_init=lam_init, tq=tq),
        lambda *a: _attn_call(_attn_online_kernel, *a, lam_init=lam_init, tq=tq),
        *attn_args)

    assert w_ffn_out.shape[1] % FFN_CHUNK == 0
    return _out_ffn_call(x, o_f, o_b, gr, od, lat(gt1), lat(sh2), lat(sc2), lat(gt2),
                         gla_norm_g[0][None, :], norm2_g[0][None, :], w_out[0].astype(BF16),
                         w_ffn_in[0].astype(BF16), w_ffn_out[0].astype(BF16), tm=tm_x)
```
